```python
import math
import jax, jax.numpy as jnp
from jax import lax
import numpy as np

D_MODEL = 1024
BATCH = 4
SEQ = 8192
DEPTH = 4

GRID_W = 64
CTX_LEN = 256
N_MIXERS = 2
N_A_LAYERS = (DEPTH + N_MIXERS - 1) // N_MIXERS
N_B_LAYERS = DEPTH // N_MIXERS
NORM_EPS = 1e-6
N_MOD = 6
CONV_W = 4
CONV_PAD = (2, 1)
LRU_WIDTH = D_MODEL
LRU_HEADS = 4
LRU_BLOCK = LRU_WIDTH // LRU_HEADS
LRU_C = 8.0
DN_QK_HEADS = 8
DN_V_HEADS = 16
DN_HEAD_K = 128
DN_HEAD_V = 128
DN_KEY_DIM = DN_QK_HEADS * DN_HEAD_K
DN_VAL_DIM = DN_V_HEADS * DN_HEAD_V
DN_CONV_DIM = 2 * DN_KEY_DIM + DN_VAL_DIM
DN_IN_DIM = DN_CONV_DIM + DN_VAL_DIM + 4 * DN_V_HEADS
DN_CHUNK = 64
D_FF = -(-8 * D_MODEL // (3 * 256)) * 256

kernel_name = 'hybrid_rglru_gdn_prefix_ctx_dit'


def rmsnorm(x, g):
    xf = x.astype(jnp.float32)
    y = xf * lax.rsqrt(jnp.mean(xf * xf, axis=-1, keepdims=True) + NORM_EPS)
    return (y * g.astype(jnp.float32)).astype(x.dtype)


def modulate(h, shift, scale):
    return h * (1 + scale) + shift


def l2norm(x):
    return x * lax.rsqrt(jnp.sum(x * x, axis=-1, keepdims=True) + NORM_EPS)


def dwconv(u, w):
    return lax.conv_general_dilated(u, w[:, None, :], window_strides=(1,), padding=[CONV_PAD],
                                    dimension_numbers=('NWC', 'WIO', 'NWC'),
                                    feature_group_count=u.shape[-1])


def to_column_major(h, n_rows):
    b, _, d = h.shape
    return h.reshape(b, n_rows, GRID_W, d).transpose(0, 2, 1, 3).reshape(b, -1, d)


def to_row_major(h, n_rows):
    b, _, d = h.shape
    return h.reshape(b, GRID_W, n_rows, d).transpose(0, 2, 1, 3).reshape(b, -1, d)


def swiglu(h, w_gu, w_down):
    gate, up = jnp.split(h @ w_gu, 2, axis=-1)
    return (jax.nn.silu(gate) * up) @ w_down


def linear_scan(a, b, h0):
    b = b.at[:, 0].add(a[:, 0] * h0)

    def combine(l, r):
        return l[0] * r[0], r[0] * l[1] + r[1]

    return lax.associative_scan(combine, (a, b), axis=1)[1]


def reverse_scan(a, b, h0):
    return jnp.flip(linear_scan(jnp.flip(a, 1), jnp.flip(b, 1), h0), 1)


def rglru_mixer(h_lat, h_ctx, w_in, b_in, conv_w, conv_b, gate_w, gate_b, lam, w_out, b_out, ctx_out):
    f32 = jnp.float32

    def branches(h):
        u, y = jnp.split(h @ w_in + b_in, 2, axis=-1)
        u = dwconv(u, conv_w) + conv_b
        return u.astype(f32), jax.nn.gelu(y)

    def coeffs(u, d):
        ub = u.reshape(u.shape[0], u.shape[1], LRU_HEADS, LRU_BLOCK)
        pre = jnp.einsum('bthi,ghij->gbthj', ub, gate_w[d].astype(f32)).reshape(2, *u.shape)
        gates = jax.nn.sigmoid(pre + gate_b[d].astype(f32)[:, None, None, :])
        r_gate, i_gate = gates[0], gates[1]
        log_a = LRU_C * r_gate * jax.nn.log_sigmoid(lam[d].astype(f32))
        a = jnp.exp(log_a)
        bx = jnp.sqrt(-jnp.expm1(2.0 * log_a)) * (i_gate * u)
        return a, bx

    u_c, y_c = branches(h_ctx)
    u_l, y_l = branches(h_lat)
    zero = jnp.zeros((h_lat.shape[0], LRU_WIDTH), f32)
    hc_f = linear_scan(*coeffs(u_c, 0), zero)
    hc_b = reverse_scan(*coeffs(u_c, 1), zero)
    hl = linear_scan(*coeffs(u_l, 0), hc_f[:, -1]) + reverse_scan(*coeffs(u_l, 1), hc_b[:, 0])
    y_lat = (hl.astype(h_lat.dtype) * y_l) @ w_out + b_out
    y_ctx = ((hc_f + hc_b).astype(h_ctx.dtype) * y_c) @ w_out + b_out if ctx_out else None
    return y_lat, y_ctx


def gated_delta_chunked(q, k, v, g, beta, s0):
    bsz, t_len, n_h, _ = q.shape
    d_v = v.shape[-1]
    n_c = t_len // DN_CHUNK

    def to_chunks(a):
        a = a.reshape(bsz, n_c, DN_CHUNK, n_h, *a.shape[3:])
        return jnp.moveaxis(a, (1, 3), (0, 2))

    incl = jnp.tril(jnp.ones((DN_CHUNK, DN_CHUNK), dtype=bool))
    strict = jnp.tril(jnp.ones((DN_CHUNK, DN_CHUNK), dtype=bool), -1)

    def step(s, xs):
        qc, kc, vc, gc, bc = xs
        gc = jnp.cumsum(gc, axis=-1)
        diff = gc[..., :, None] - gc[..., None, :]
        decay = jnp.where(incl, jnp.exp(jnp.where(incl, diff, 0.0)), 0.0)
        kb = kc * bc[..., None]
        m = jnp.where(strict, jnp.einsum('bhid,bhjd->bhij', kb, kc) * decay, 0.0)
        rhs = jnp.concatenate([vc * bc[..., None], kb * jnp.exp(gc)[..., None]], axis=-1)
        sol = lax.linalg.triangular_solve(m, rhs, left_side=True, lower=True, unit_diagonal=True)
        u, w = sol[..., :d_v], sol[..., d_v:]
        v_new = u - jnp.einsum('bhcd,bhde->bhce', w, s)
        attn = jnp.where(incl, jnp.einsum('bhid,bhjd->bhij', qc, kc) * decay, 0.0)
        o = (jnp.einsum('bhcd,bhde->bhce', qc * jnp.exp(gc)[..., None], s)
             + jnp.einsum('bhij,bhje->bhie', attn, v_new))
        g_last = gc[..., -1]
        s = (s * jnp.exp(g_last)[..., None, None]
             + jnp.einsum('bhcd,bhce->bhde', kc * jnp.exp(g_last[..., None] - gc)[..., None], v_new))
        return s, o

    s_fin, o = lax.scan(step, s0, tuple(map(to_chunks, (q, k, v, g, beta))))
    o = jnp.moveaxis(o, (0, 2), (1, 3)).reshape(bsz, t_len, n_h, d_v)
    return o, s_fin


def run_delta(q, k, v, g, beta, s0, reverse):
    if reverse:
        q, k, v, g, beta = (jnp.flip(a, 1) for a in (q, k, v, g, beta))
        o, s = gated_delta_chunked(q, k, v, g, beta, s0)
        return jnp.flip(o, 1), s
    return gated_delta_chunked(q, k, v, g, beta, s0)


def deltanet_mixer(h_lat, h_ctx, w_in, conv_w, a_log, dt_bias, norm_w, w_out, ctx_out):
    f32 = jnp.float32
    rep = DN_V_HEADS // DN_QK_HEADS

    def prep(h):
        bsz, t_len = h.shape[:2]
        qkv, z, ba = jnp.split(h @ w_in, [DN_CONV_DIM, DN_CONV_DIM + DN_VAL_DIM], axis=-1)
        qkv = jax.nn.silu(dwconv(qkv, conv_w).astype(f32))
        q, k, v = jnp.split(qkv, [DN_KEY_DIM, 2 * DN_KEY_DIM], axis=-1)
        q = jnp.repeat(l2norm(q.reshape(bsz, t_len, DN_QK_HEADS, DN_HEAD_K)), rep, axis=2) * DN_HEAD_K ** -0.5
        k = jnp.repeat(l2norm(k.reshape(bsz, t_len, DN_QK_HEADS, DN_HEAD_K)), rep, axis=2)
        v = v.reshape(bsz, t_len, DN_V_HEADS, DN_HEAD_V)
        ba = ba.astype(f32).reshape(bsz, t_len, 2, 2, DN_V_HEADS)
        beta = jax.nn.sigmoid(ba[:, :, 0])
        g = -jnp.exp(a_log.astype(f32)) * jax.nn.softplus(ba[:, :, 1] + dt_bias.astype(f32))
        return (q, k, v, g, beta), z

    def gated_out(o, z):
        bsz, t_len = o.shape[:2]
        o = o * lax.rsqrt(jnp.mean(o * o, axis=-1, keepdims=True) + NORM_EPS) * norm_w.astype(f32)
        o = o * jax.nn.silu(z.astype(f32).reshape(bsz, t_len, DN_V_HEADS, DN_HEAD_V))
        return o.reshape(bsz, t_len, DN_VAL_DIM).astype(w_out.dtype) @ w_out

    def direction(t, d):
        q, k, v, g, beta = t
        return q, k, v, g[:, :, d], beta[:, :, d]

    c_in, z_c = prep(h_ctx)
    l_in, z_l = prep(h_lat)
    s0 = jnp.zeros((h_lat.shape[0], DN_V_HEADS, DN_HEAD_K, DN_HEAD_V), f32)
    oc_f, sc_f = run_delta(*direction(c_in, 0), s0, False)
    oc_b, sc_b = run_delta(*direction(c_in, 1), s0, True)
    ol_f, _ = run_delta(*direction(l_in, 0), sc_f, False)
    ol_b, _ = run_delta(*direction(l_in, 1), sc_b, True)
    y_lat = gated_out(ol_f + ol_b, z_l).astype(h_lat.dtype)
    y_ctx = gated_out(oc_f + oc_b, z_c).astype(h_ctx.dtype) if ctx_out else None
    return y_lat, y_ctx


def setup_inputs(seed: int = 0) -> dict:
    key = jax.random.key(seed)
    ks = jax.random.split(key, 26)
    f32 = jnp.float32
    D = D_MODEL

    def nrm(k, shape, scale):
        return scale * jax.random.normal(k, shape, f32)

    lam_u = jax.random.uniform(ks[12], (N_A_LAYERS, 2, LRU_WIDTH), f32, 0.9, 0.999)
    lam_a = lam_u ** (1.0 / LRU_C)
    dt = jnp.exp(jax.random.uniform(ks[17], (N_B_LAYERS, 2, DN_V_HEADS), f32, math.log(1e-3), math.log(1e-1)))
    return {
        'x': nrm(ks[0], (BATCH, SEQ, D), 1.0),
        'c': nrm(ks[1], (BATCH, D), 1.0),
        'ctx': nrm(ks[2], (BATCH, CTX_LEN, D), 1.0),
        'c_ctx': nrm(ks[3], (D,), 1.0),
        'ada_w': nrm(ks[4], (DEPTH, D, N_MOD * D), D ** -0.5),
        'ada_b': nrm(ks[5], (DEPTH, N_MOD * D), 0.01),
        'norm_g': 1.0 + nrm(ks[6], (DEPTH, 2, D), 0.02),
        'final_norm_g': 1.0 + nrm(ks[7], (D,), 0.02),
        'rg_w_in': nrm(ks[8], (N_A_LAYERS, D, 2 * LRU_WIDTH), D ** -0.5),
        'rg_b_in': nrm(ks[9], (N_A_LAYERS, 2 * LRU_WIDTH), 0.01),
        'rg_conv_w': nrm(ks[10], (N_A_LAYERS, CONV_W, LRU_WIDTH), CONV_W ** -0.5),
        'rg_conv_b': nrm(ks[11], (N_A_LAYERS, LRU_WIDTH), 0.01),
        'rg_gate_w': nrm(ks[13], (N_A_LAYERS, 2, 2, LRU_HEADS, LRU_BLOCK, LRU_BLOCK), LRU_BLOCK ** -0.5),
        'rg_gate_b': nrm(ks[14], (N_A_LAYERS, 2, 2, LRU_WIDTH), 0.01),
        'rg_lambda': jnp.log(lam_a) - jnp.log1p(-lam_a),
        'rg_w_out': nrm(ks[15], (N_A_LAYERS, LRU_WIDTH, D), LRU_WIDTH ** -0.5),
        'rg_b_out': nrm(ks[16], (N_A_LAYERS, D), 0.01),
        'dn_w_in': nrm(ks[18], (N_B_LAYERS, D, DN_IN_DIM), D ** -0.5),
        'dn_conv_w': nrm(ks[19], (N_B_LAYERS, CONV_W, DN_CONV_DIM), CONV_W ** -0.5),
        'dn_a_log': jnp.log(jax.random.uniform(ks[20], (N_B_LAYERS, 2, DN_V_HEADS), f32, 1.0, 16.0)),
        'dn_dt_bias': dt + jnp.log(-jnp.expm1(-dt)),
        'dn_norm_w': 1.0 + nrm(ks[21], (N_B_LAYERS, DN_HEAD_V), 0.02),
        'dn_w_out': nrm(ks[22], (N_B_LAYERS, DN_VAL_DIM, D), DN_VAL_DIM ** -0.5),
        'ffn_w_gu': nrm(ks[23], (DEPTH, D, 2 * D_FF), D ** -0.5),
        'ffn_w_down': nrm(ks[24], (DEPTH, D_FF, D), D_FF ** -0.5),
    }


def reference(x, c, ctx, c_ctx, ada_w, ada_b, norm_g, final_norm_g, rg_w_in, rg_b_in, rg_conv_w,
              rg_conv_b, rg_gate_w, rg_gate_b, rg_lambda, rg_w_out, rg_b_out, dn_w_in, dn_conv_w,
              dn_a_log, dn_dt_bias, dn_norm_w, dn_w_out, ffn_w_gu, ffn_w_down):
    n_rows = x.shape[1] // GRID_W
    lat_act = jax.nn.silu(c)
    ctx_act = jax.nn.silu(c_ctx)
    for layer in range(DEPTH):
        last = layer == DEPTH - 1
        mod_l = jnp.split((lat_act @ ada_w[layer] + ada_b[layer])[:, None, :], N_MOD, axis=-1)
        mod_c = jnp.split(ctx_act @ ada_w[layer] + ada_b[layer], N_MOD, axis=-1)
        h_lat = modulate(rmsnorm(x, norm_g[layer, 0]), mod_l[0], mod_l[1])
        h_ctx = modulate(rmsnorm(ctx, norm_g[layer, 0]), mod_c[0], mod_c[1])
        j = layer // N_MIXERS
        if layer % N_MIXERS == 0:
            y_lat, y_ctx = rglru_mixer(h_lat, h_ctx, rg_w_in[j], rg_b_in[j], rg_conv_w[j], rg_conv_b[j],
                                       rg_gate_w[j], rg_gate_b[j], rg_lambda[j], rg_w_out[j], rg_b_out[j],
                                       not last)
        else:
            y_lat, y_ctx = deltanet_mixer(to_column_major(h_lat, n_rows), h_ctx, dn_w_in[j], dn_conv_w[j],
                                          dn_a_log[j], dn_dt_bias[j], dn_norm_w[j], dn_w_out[j], not last)
            y_lat = to_row_major(y_lat, n_rows)
        x = x + mod_l[2] * y_lat
        x = x + mod_l[5] * swiglu(modulate(rmsnorm(x, norm_g[layer, 1]), mod_l[3], mod_l[4]),
                                  ffn_w_gu[layer], ffn_w_down[layer])
        if not last:
            ctx = ctx + mod_c[2] * y_ctx
            ctx = ctx + mod_c[5] * swiglu(modulate(rmsnorm(ctx, norm_g[layer, 1]), mod_c[3], mod_c[4]),
                                          ffn_w_gu[layer], ffn_w_down[layer])
    return rmsnorm(x, final_norm_g)
```

```python
import functools

import jax
import jax.numpy as jnp
from jax import lax
from jax.experimental import pallas as pl
from jax.experimental.pallas import tpu as pltpu

F32 = jnp.float32
BF16 = jnp.bfloat16

D_MODEL = 1024
DEPTH = 4
GRID_W = 64
N_MOD = 6
NORM_EPS = 1e-6
CONV_W = 4
LRU_HEADS = 4
LRU_BLOCK = D_MODEL // LRU_HEADS
LRU_C = 8.0
DN_QK_HEADS = 8
DN_V_HEADS = 16
DN_HEAD = 128
DN_KEY_DIM = DN_QK_HEADS * DN_HEAD
DN_VAL_DIM = DN_V_HEADS * DN_HEAD
DN_CONV_DIM = 2 * DN_KEY_DIM + DN_VAL_DIM
D_FF = 2816

SUBLANES = 8
LANES = 128
HALO = SUBLANES
CHUNK = 128
MOD_ROWS = 8
CTX_MOD_ROW = 4
VMEM_LIMIT_BYTES = 56 * 1024 * 1024
FF_CHUNKS = ((0, 1536), (1536, 1280))


def _cparams(n_axes):
    return pltpu.CompilerParams(dimension_semantics=("arbitrary",) * n_axes,
                                vmem_limit_bytes=VMEM_LIMIT_BYTES)


def _whole(shape):
    nd = len(shape)
    return pl.BlockSpec(shape, lambda *_: (0,) * nd, pipeline_mode=pl.Buffered(1))


def _dot(a, b):
    return jnp.dot(a, b, preferred_element_type=F32)


def _silu(v):
    return v * jax.nn.sigmoid(v)


def _softplus(v):
    return jnp.maximum(v, 0.0) + jnp.log1p(jnp.exp(-jnp.abs(v)))


def _log_sigmoid(v):
    return jnp.minimum(v, 0.0) - jnp.log1p(jnp.exp(-jnp.abs(v)))


def _rmsnorm(x, g):
    return x * lax.rsqrt(jnp.mean(x * x, axis=-1, keepdims=True) + NORM_EPS) * g


def _norm_mod(x, g, shift, scale):
    return _rmsnorm(x, g) * (1.0 + scale) + shift


def _ada_kernel(act_ref, w_ref, b_ref, o_ref):
    a = _silu(act_ref[...])
    o_ref[0] = _dot(a, w_ref[0]) + b_ref[0]


def _ada_call(act, ada_w, ada_b):
    n_col = N_MOD * D_MODEL // D_MODEL
    return pl.pallas_call(
        _ada_kernel,
        grid=(DEPTH, n_col),
        in_specs=[pl.BlockSpec((MOD_ROWS, D_MODEL), lambda l, j: (0, 0)),
                  pl.BlockSpec((1, D_MODEL, D_MODEL), lambda l, j: (l, 0, j)),
                  pl.BlockSpec((1, 1, D_MODEL), lambda l, j: (l, 0, j))],
        out_specs=pl.BlockSpec((1, MOD_ROWS, D_MODEL), lambda l, j: (l, 0, j)),
        out_shape=jax.ShapeDtypeStruct((DEPTH, MOD_ROWS, N_MOD * D_MODEL), F32),
        compiler_params=_cparams(2),
        name="ada",
    )(act, ada_w, ada_b.reshape(DEPTH, 1, N_MOD * D_MODEL))


def _ffn_kernel(x_ref, mod_ref, g_ref, wgu_ref, wd_ref, *rest, final):
    if final:
        fg_ref, o_ref = rest
    else:
        (o_ref,) = rest
    x = x_ref[0]
    h = _norm_mod(x, g_ref[...], mod_ref[0, 3:4, :], mod_ref[0, 4:5, :]).astype(BF16)
    acc = None
    for c0, cw in FF_CHUNKS:
        gate = _dot(h, wgu_ref[:, c0:c0 + cw])
        up = _dot(h, wgu_ref[:, D_FF + c0:D_FF + c0 + cw])
        part = _dot((_silu(gate) * up).astype(BF16), wd_ref[c0:c0 + cw, :])
        acc = part if acc is None else acc + part
    y = x + mod_ref[0, 5:6, :] * acc
    if final:
        y = _rmsnorm(y, fg_ref[...])
    o_ref[0] = y


def _mod_spec(is_ctx):
    if is_ctx:
        return pl.BlockSpec((1, MOD_ROWS, D_MODEL), lambda b, i: (CTX_MOD_ROW, 0, 0))
    return pl.BlockSpec((1, MOD_ROWS, D_MODEL), lambda b, i: (b, 0, 0))


def _tok_spec(tm, width):
    return pl.BlockSpec((1, tm, width), lambda b, i: (b, i, 0))


def _ffn_call(x, mod, g, wgu, wd, final_g, *, tm, is_ctx):
    bsz, t_len, _ = x.shape
    final = final_g is not None
    in_specs = [_tok_spec(tm, D_MODEL), _mod_spec(is_ctx), _whole((1, D_MODEL)),
                _whole(wgu.shape), _whole(wd.shape)]
    args = [x, mod, g, wgu, wd]
    if final:
        in_specs.append(_whole((1, D_MODEL)))
        args.append(final_g)
    return pl.pallas_call(
        functools.partial(_ffn_kernel, final=final),
        grid=(bsz, t_len // tm),
        in_specs=in_specs,
        out_specs=_tok_spec(tm, D_MODEL),
        out_shape=jax.ShapeDtypeStruct(x.shape, F32),
        compiler_params=_cparams(2),
        name="ffn",
    )(*args)


def _halo_specs(tm, t_len):
    per = tm // HALO
    last = t_len // HALO - 1
    prev = pl.BlockSpec((1, HALO, D_MODEL), lambda b, i: (b, jnp.maximum(i * per - 1, 0), 0))
    nxt = pl.BlockSpec((1, HALO, D_MODEL), lambda b, i: (b, jnp.minimum((i + 1) * per, last), 0))
    return prev, nxt


def _load_normed_with_halo(xm_ref, xp_ref, xn_ref, xcat, g, shift, scale, tm):
    xcat[0:HALO, :] = xp_ref[0]
    xcat[HALO:HALO + tm, :] = xm_ref[0]
    xcat[HALO + tm:2 * HALO + tm, :] = xn_ref[0]
    return _norm_mod(xcat[...], g, shift, scale).astype(BF16)


def _halo_keep_mask(tm, i, nt):
    row = lax.broadcasted_iota(jnp.int32, (tm + 2 * HALO, 1), 0)
    keep_lo = jnp.logical_or(row >= HALO, i > 0)
    keep_hi = jnp.logical_or(row < HALO + tm, i < nt - 1)
    return jnp.logical_and(keep_lo, keep_hi)


def _dwconv_from(pbuf, cw, tm):
    out = None
    for j in range(CONV_W):
        term = cw[j:j + 1, :] * pbuf[pl.ds(HALO - 2 + j, tm), :]
        out = term if out is None else out + term
    return out


def _group_iota():
    return lax.broadcasted_iota(jnp.int32, (1, SUBLANES, 1), 1)


def _local_scan(a, b, tm, reverse):
    a3 = a.reshape(tm // SUBLANES, SUBLANES, D_MODEL)
    b3 = b.reshape(tm // SUBLANES, SUBLANES, D_MODEL)
    r = _group_iota()
    for s in (1, 2, 4):
        shift = SUBLANES - s if reverse else s
        a_sh = pltpu.roll(a3, shift, 1)
        b_sh = pltpu.roll(b3, shift, 1)
        valid = (r < SUBLANES - s) if reverse else (r >= s)
        b3 = jnp.where(valid, a3 * b_sh + b3, b3)
        a3 = jnp.where(valid, a3 * a_sh, a3)
    return a3.reshape(tm, D_MODEL), b3.reshape(tm, D_MODEL)


def _carry_scan(a_s, b_s, write, h, tm, reverse):
    n_grp = tm // SUBLANES

    def body(t, h):
        g = (n_grp - 1 - t) if reverse else t
        r0 = pl.multiple_of(g * SUBLANES, SUBLANES)
        hh = a_s[pl.ds(r0, SUBLANES), :] * h + b_s[pl.ds(r0, SUBLANES), :]
        write(r0, hh)
        return hh[0:1, :] if reverse else hh[SUBLANES - 1:SUBLANES, :]

    return lax.fori_loop(0, n_grp, body, h, unroll=8)


def _rg_a_kernel(xm_ref, xp_ref, xn_ref, mod_ref, g_ref, wu_ref, wy_ref, bin_ref, cw_ref, cb_ref,
                 gw_ref, gb_ref, lam_ref, h0_ref,
                 hf_ref, ab_ref, bxb_ref, y_ref, hfin_ref,
                 xcat, pbuf, a_s, b_s, carry, *, tm, nt):
    i = pl.program_id(1)
    h = _load_normed_with_halo(xm_ref, xp_ref, xn_ref, xcat, g_ref[...],
                               mod_ref[0, 0:1, :], mod_ref[0, 1:2, :], tm)
    pu = _dot(h, wu_ref[...]) + bin_ref[:, 0:D_MODEL]
    pbuf[...] = jnp.where(_halo_keep_mask(tm, i, nt), pu, 0.0)
    u = _dwconv_from(pbuf, cw_ref[...], tm) + cb_ref[...]
    py = _dot(h[HALO:HALO + tm, :], wy_ref[...]) + bin_ref[:, D_MODEL:2 * D_MODEL]
    y_ref[0] = jax.nn.gelu(py).astype(BF16)

    ub = u.astype(BF16)
    log_sig_lam = _log_sigmoid(lam_ref[...])
    for d in (0, 1):
        pre = []
        for gi in (0, 1):
            parts = [_dot(ub[:, hh * LRU_BLOCK:(hh + 1) * LRU_BLOCK], gw_ref[d, gi, hh])
                     for hh in range(LRU_HEADS)]
            pre.append(jnp.concatenate(parts, axis=1) + gb_ref[d, gi:gi + 1, :])
        r_gate = jax.nn.sigmoid(pre[0])
        i_gate = jax.nn.sigmoid(pre[1])
        log_a = LRU_C * r_gate * log_sig_lam[d:d + 1, :]
        a = jnp.exp(log_a)
        t = jnp.tanh(log_a)
        bx = jnp.sqrt(-2.0 * t / (1.0 - t)) * (i_gate * u)
        if d == 1:
            ab_ref[0] = a
            bxb_ref[0] = bx
        else:
            a_loc, b_loc = _local_scan(a, bx, tm, reverse=False)
            a_s[...] = a_loc
            b_s[...] = b_loc

    @pl.when(i == 0)
    def _():
        carry[...] = h0_ref[0]

    def write(r0, hh):
        hf_ref[0, pl.ds(r0, SUBLANES), :] = hh

    h_end = _carry_scan(a_s, b_s, write, carry[0:1, :], tm, reverse=False)
    carry[...] = jnp.broadcast_to(h_end, (SUBLANES, D_MODEL))

    @pl.when(i == nt - 1)
    def _():
        hfin_ref[0] = carry[...]


def _rg_a_call(x, mod, g, wu, wy, b_in, cw, cb, gw, gb, lam, h0, *, tm, is_ctx):
    bsz, t_len, _ = x.shape
    nt = t_len // tm
    prev, nxt = _halo_specs(tm, t_len)
    state_spec = pl.BlockSpec((1, SUBLANES, D_MODEL), lambda b, i: (b, 0, 0))
    tok = _tok_spec(tm, D_MODEL)
    return pl.pallas_call(
        functools.partial(_rg_a_kernel, tm=tm, nt=nt),
        grid=(bsz, nt),
        in_specs=[tok, prev, nxt, _mod_spec(is_ctx), _whole((1, D_MODEL)),
                  _whole(wu.shape), _whole(wy.shape), _whole(b_in.shape), _whole(cw.shape),
                  _whole(cb.shape), _whole(gw.shape), _whole(gb.shape), _whole(lam.shape),
                  state_spec],
        out_specs=[tok, tok, tok, tok, state_spec],
        out_shape=[jax.ShapeDtypeStruct(x.shape, F32), jax.ShapeDtypeStruct(x.shape, F32),
                   jax.ShapeDtypeStruct(x.shape, F32), jax.ShapeDtypeStruct(x.shape, BF16),
                   jax.ShapeDtypeStruct((bsz, SUBLANES, D_MODEL), F32)],
        scratch_shapes=[pltpu.VMEM((tm + 2 * HALO, D_MODEL), F32),
                        pltpu.VMEM((tm + 2 * HALO, D_MODEL), F32),
                        pltpu.VMEM((tm, D_MODEL), F32), pltpu.VMEM((tm, D_MODEL), F32),
                        pltpu.VMEM((SUBLANES, D_MODEL), F32)],
        compiler_params=_cparams(2),
        name="rg_a",
    )(x, x, x, mod, g, wu, wy, b_in, cw, cb, gw, gb, lam, h0)


def _rg_b_kernel(x_ref, hf_ref, ab_ref, bxb_ref, y_ref, h0_ref, mod_ref, wo_ref, bo_ref,
                 o_ref, hfin_ref, a_s, b_s, hb_s, carry, *, tm, nt):
    i = pl.program_id(1)
    a_loc, b_loc = _local_scan(ab_ref[0], bxb_ref[0], tm, reverse=True)
    a_s[...] = a_loc
    b_s[...] = b_loc

    @pl.when(i == 0)
    def _():
        carry[...] = h0_ref[0]

    def write(r0, hh):
        hb_s[pl.ds(r0, SUBLANES), :] = hh

    h_end = _carry_scan(a_s, b_s, write, carry[0:1, :], tm, reverse=True)
    carry[...] = jnp.broadcast_to(h_end, (SUBLANES, D_MODEL))

    @pl.when(i == nt - 1)
    def _():
        hfin_ref[0] = carry[...]

    hl = hf_ref[0] + hb_s[...]
    m = (hl * y_ref[0].astype(F32)).astype(BF16)
    out = _dot(m, wo_ref[...]) + bo_ref[...]
    o_ref[0] = x_ref[0] + mod_ref[0, 2:3, :] * out


def _rg_b_call(x, hf, ab, bxb, y, h0, mod, wo, bo, *, tm, is_ctx):
    bsz, t_len, _ = x.shape
    nt = t_len // tm
    tok = pl.BlockSpec((1, tm, D_MODEL), lambda b, i: (b, nt - 1 - i, 0))
    state_spec = pl.BlockSpec((1, SUBLANES, D_MODEL), lambda b, i: (b, 0, 0))
    return pl.pallas_call(
        functools.partial(_rg_b_kernel, tm=tm, nt=nt),
        grid=(bsz, nt),
        in_specs=[tok, tok, tok, tok, tok, state_spec, _mod_spec(is_ctx),
                  _whole(wo.shape), _whole(bo.shape)],
        out_specs=[tok, state_spec],
        out_shape=[jax.ShapeDtypeStruct(x.shape, F32),
                   jax.ShapeDtypeStruct((bsz, SUBLANES, D_MODEL), F32)],
        scratch_shapes=[pltpu.VMEM((tm, D_MODEL), F32), pltpu.VMEM((tm, D_MODEL), F32),
                        pltpu.VMEM((tm, D_MODEL), F32), pltpu.VMEM((SUBLANES, D_MODEL), F32)],
        compiler_params=_cparams(2),
        name="rg_b",
    )(x, hf, ab, bxb, y, h0, mod, wo, bo)


def _dn_prep_kernel(xm_ref, xp_ref, xn_ref, mod_ref, g_ref, wqkv_ref, wz_ref, wba_ref, cw_ref,
                    alog_ref, dtb_ref,
                    q_ref, k_ref, v_ref, z_ref, gb_ref,
                    xcat, pbuf, *, tm, nt):
    i = pl.program_id(1)
    h = _load_normed_with_halo(xm_ref, xp_ref, xn_ref, xcat, g_ref[...],
                               mod_ref[0, 0:1, :], mod_ref[0, 1:2, :], tm)
    keep = _halo_keep_mask(tm, i, nt)
    n_slab = DN_CONV_DIM // D_MODEL
    for c in range(n_slab):
        cols = slice(c * D_MODEL, (c + 1) * D_MODEL)
        pbuf[...] = jnp.where(keep, _dot(h, wqkv_ref[:, cols]), 0.0)
        u = _silu(_dwconv_from(pbuf, cw_ref[:, cols], tm))
        if c < 2:
            scale = DN_HEAD ** -0.5 if c == 0 else 1.0
            dst = q_ref if c == 0 else k_ref
            for hh in range(DN_QK_HEADS):
                uh = u[:, hh * DN_HEAD:(hh + 1) * DN_HEAD]
                uh = uh * lax.rsqrt(jnp.sum(uh * uh, axis=-1, keepdims=True) + NORM_EPS)
                if c == 0:
                    uh = uh * scale
                dst[0, :, hh * DN_HEAD:(hh + 1) * DN_HEAD] = uh.astype(BF16)
        else:
            v_ref[0, :, (c - 2) * D_MODEL:(c - 1) * D_MODEL] = u.astype(BF16)

    hm = h[HALO:HALO + tm, :]
    z_ref[0] = _dot(hm, wz_ref[...]).astype(BF16)
    ba = _dot(hm, wba_ref[...])
    lane = lax.broadcasted_iota(jnp.int32, (1, LANES), 1)
    beta = jax.nn.sigmoid(ba)
    gdec = -jnp.exp(alog_ref[...]) * _softplus(ba + dtb_ref[...])
    n_bd = 2 * DN_V_HEADS
    gb_ref[0] = jnp.where(lane < n_bd, beta, jnp.where(lane < 2 * n_bd, gdec, 0.0))


def _dn_prep_call(x, mod, g, wqkv, wz, wba, cw, alog, dtb, *, tm, is_ctx):
    bsz, t_len, _ = x.shape
    nt = t_len // tm
    prev, nxt = _halo_specs(tm, t_len)
    return pl.pallas_call(
        functools.partial(_dn_prep_kernel, tm=tm, nt=nt),
        grid=(bsz, nt),
        in_specs=[_tok_spec(tm, D_MODEL), prev, nxt, _mod_spec(is_ctx), _whole((1, D_MODEL)),
                  _whole(wqkv.shape), _whole(wz.shape), _whole(wba.shape), _whole(cw.shape),
                  _whole(alog.shape), _whole(dtb.shape)],
        out_specs=[_tok_spec(tm, DN_KEY_DIM), _tok_spec(tm, DN_KEY_DIM), _tok_spec(tm, DN_VAL_DIM),
                   _tok_spec(tm, DN_VAL_DIM), _tok_spec(tm, LANES)],
        out_shape=[jax.ShapeDtypeStruct((bsz, t_len, DN_KEY_DIM), BF16),
                   jax.ShapeDtypeStruct((bsz, t_len, DN_KEY_DIM), BF16),
                   jax.ShapeDtypeStruct((bsz, t_len, DN_VAL_DIM), BF16),
                   jax.ShapeDtypeStruct((bsz, t_len, DN_VAL_DIM), BF16),
                   jax.ShapeDtypeStruct((bsz, t_len, LANES), F32)],
        scratch_shapes=[pltpu.VMEM((tm + 2 * HALO, D_MODEL), F32),
                        pltpu.VMEM((tm + 2 * HALO, D_MODEL), F32)],
        compiler_params=_cparams(2),
        name="dn_prep",
    )(x, x, x, mod, g, wqkv, wz, wba, cw, alog, dtb)


def _tri_inverse(m, row, col):
    x = jnp.where(row == col, 1.0, 0.0) - jnp.where((row >> 1) == (col >> 1), m, 0.0)
    lvl = 1
    while (1 << lvl) < CHUNK:
        same = (row >> lvl) == (col >> lvl)
        same2 = (row >> (lvl + 1)) == (col >> (lvl + 1))
        m_off = jnp.where(jnp.logical_and(same2, jnp.logical_not(same)), m, 0.0).astype(BF16)
        xb = x.astype(BF16)
        x = x - _dot(_dot(xb, m_off).astype(BF16), xb)
        lvl += 1
    return x


def _dn_core_kernel(q_ref, k_ref, v_ref, gb_ref, s0_ref, o_ref, sfin_ref, s_ref, *, nc, reverse):
    i = pl.program_id(1)

    @pl.when(i == 0)
    def _():
        s_ref[...] = s0_ref[0]

    row = lax.broadcasted_iota(jnp.int32, (CHUNK, CHUNK), 0)
    col = lax.broadcasted_iota(jnp.int32, (CHUNK, CHUNK), 1)
    incl = (row <= col) if reverse else (row >= col)
    strict = (row < col) if reverse else (row > col)
    gb = gb_ref[0]
    gc_all = jnp.dot(incl.astype(F32), gb, preferred_element_type=F32,
                     precision=lax.Precision.HIGHEST)
    gc_t = gc_all.T
    d = 1 if reverse else 0
    last = 0 if reverse else CHUNK - 1
    nt_dims = (((1,), (1,)), ((), ()))
    tn_dims = (((0,), (0,)), ((), ()))
    kk = {}
    qk = {}
    for hv in range(DN_V_HEADS):
        hq = hv // (DN_V_HEADS // DN_QK_HEADS)
        qh = q_ref[0, :, hq * DN_HEAD:(hq + 1) * DN_HEAD]
        kh = k_ref[0, :, hq * DN_HEAD:(hq + 1) * DN_HEAD]
        vh = v_ref[0, :, hv * DN_HEAD:(hv + 1) * DN_HEAD].astype(F32)
        if hq not in kk:
            kk[hq] = lax.dot_general(kh, kh, nt_dims, preferred_element_type=F32)
            qk[hq] = lax.dot_general(qh, kh, nt_dims, preferred_element_type=F32)
        col_b = DN_V_HEADS * d + hv
        col_g = 2 * DN_V_HEADS + DN_V_HEADS * d + hv
        beta_c = gb[:, col_b:col_b + 1]
        gc_c = gc_all[:, col_g:col_g + 1]
        gc_r = gc_t[col_g:col_g + 1, :]
        g_last = gc_c[last:last + 1, :]
        decay = jnp.where(incl, jnp.exp(jnp.where(incl, gc_c - gc_r, 0.0)), 0.0)
        m = jnp.where(strict, kk[hq] * beta_c * decay, 0.0)
        a_inv = _tri_inverse(m, row, col)
        attn = jnp.where(incl, qk[hq] * decay, 0.0)
        eg = jnp.exp(gc_c)
        khf = kh.astype(F32)
        kg = (khf * beta_c * eg).astype(BF16)
        vb = vh * beta_c
        qg = (qh.astype(F32) * eg).astype(BF16)
        kd = (khf * jnp.exp(g_last - gc_c)).astype(BF16)
        s = s_ref[hv]
        sb = s.astype(BF16)
        resid = vb - _dot(kg, sb)
        v_new = _dot(a_inv.astype(BF16), resid.astype(BF16)).astype(BF16)
        o = _dot(qg, sb) + _dot(attn.astype(BF16), v_new)
        s_ref[hv] = s * jnp.exp(g_last) + lax.dot_general(kd, v_new, tn_dims,
                                                          preferred_element_type=F32)
        o_ref[0, :, hv * DN_HEAD:(hv + 1) * DN_HEAD] = o

    @pl.when(i == nc - 1)
    def _():
        sfin_ref[0] = s_ref[...]


def _dn_core_call(q, k, v, gb, s0, *, reverse):
    bsz, t_len, _ = q.shape
    nc = t_len // CHUNK
    if reverse:
        imap = lambda b, i: (b, nc - 1 - i, 0)
    else:
        imap = lambda b, i: (b, i, 0)
    state_spec = pl.BlockSpec((1, DN_V_HEADS, DN_HEAD, DN_HEAD), lambda b, i: (b, 0, 0, 0))
    return pl.pallas_call(
        functools.partial(_dn_core_kernel, nc=nc, reverse=reverse),
        grid=(bsz, nc),
        in_specs=[pl.BlockSpec((1, CHUNK, DN_KEY_DIM), imap), pl.BlockSpec((1, CHUNK, DN_KEY_DIM), imap),
                  pl.BlockSpec((1, CHUNK, DN_VAL_DIM), imap), pl.BlockSpec((1, CHUNK, LANES), imap),
                  state_spec],
        out_specs=[pl.BlockSpec((1, CHUNK, DN_VAL_DIM), imap), state_spec],
        out_shape=[jax.ShapeDtypeStruct((bsz, t_len, DN_VAL_DIM), F32),
                   jax.ShapeDtypeStruct((bsz, DN_V_HEADS, DN_HEAD, DN_HEAD), F32)],
        scratch_shapes=[pltpu.VMEM((DN_V_HEADS, DN_HEAD, DN_HEAD), F32)],
        compiler_params=_cparams(2),
        name="dn_core_bwd" if reverse else "dn_core_fwd",
    )(q, k, v, gb, s0)


def _dn_out_kernel(of_ref, ob_ref, z_ref, x_ref, mod_ref, nw_ref, wo_ref, o_ref):
    o = of_ref[0] + ob_ref[0]
    parts = []
    for hh in range(DN_V_HEADS):
        cols = slice(hh * DN_HEAD, (hh + 1) * DN_HEAD)
        oh = o[:, cols]
        oh = oh * lax.rsqrt(jnp.mean(oh * oh, axis=-1, keepdims=True) + NORM_EPS) * nw_ref[...]
        parts.append((oh * _silu(z_ref[0, :, cols].astype(F32))).astype(BF16))
    y = _dot(jnp.concatenate(parts, axis=1), wo_ref[...])
    o_ref[0] = x_ref[0] + mod_ref[0, 2:3, :] * y


def _dn_out_call(of, ob, z, x, mod, nw, wo, *, tm, is_ctx):
    bsz, t_len, _ = x.shape
    return pl.pallas_call(
        _dn_out_kernel,
        grid=(bsz, t_len // tm),
        in_specs=[_tok_spec(tm, DN_VAL_DIM), _tok_spec(tm, DN_VAL_DIM), _tok_spec(tm, DN_VAL_DIM),
                  _tok_spec(tm, D_MODEL), _mod_spec(is_ctx), _whole(nw.shape), _whole(wo.shape)],
        out_specs=_tok_spec(tm, D_MODEL),
        out_shape=jax.ShapeDtypeStruct(x.shape, F32),
        compiler_params=_cparams(2),
        name="dn_out",
    )(of, ob, z, x, mod, nw, wo)


def _to_column_major(h, n_rows):
    b, _, d = h.shape
    return h.reshape(b, n_rows, GRID_W, d).transpose(0, 2, 1, 3).reshape(b, -1, d)


def _to_row_major(h, n_rows):
    b, _, d = h.shape
    return h.reshape(b, GRID_W, n_rows, d).transpose(0, 2, 1, 3).reshape(b, -1, d)


def _pad_lanes(a, width=LANES):
    return jnp.pad(a, [(0, 0)] * (a.ndim - 1) + [(0, width - a.shape[-1])])


TM_LAT = 512
TM_CTX = 256


def kernel(x, c, ctx, c_ctx, ada_w, ada_b, norm_g, final_norm_g, rg_w_in, rg_b_in, rg_conv_w,
           rg_conv_b, rg_gate_w, rg_gate_b, rg_lambda, rg_w_out, rg_b_out, dn_w_in, dn_conv_w,
           dn_a_log, dn_dt_bias, dn_norm_w, dn_w_out, ffn_w_gu, ffn_w_down):
    bsz, t_len, _ = x.shape
    n_rows = t_len // GRID_W
    act = jnp.zeros((MOD_ROWS, D_MODEL), F32).at[0:bsz].set(c).at[CTX_MOD_ROW].set(c_ctx)
    mods = _ada_call(act, ada_w, ada_b).reshape(DEPTH, MOD_ROWS, N_MOD, D_MODEL)
    mods = jnp.pad(mods, ((0, 0), (0, 0), (0, MOD_ROWS - N_MOD), (0, 0)))
    zero_state = jnp.zeros((bsz, SUBLANES, D_MODEL), F32)
    zero_s = jnp.zeros((bsz, DN_V_HEADS, DN_HEAD, DN_HEAD), F32)

    for layer in range(DEPTH):
        last = layer == DEPTH - 1
        mod = mods[layer]
        g1 = norm_g[layer, 0].reshape(1, D_MODEL)
        g2 = norm_g[layer, 1].reshape(1, D_MODEL)
        wgu = ffn_w_gu[layer].astype(BF16)
        wd = ffn_w_down[layer].astype(BF16)
        j = layer // 2
        if layer % 2 == 0:
            wu = rg_w_in[j][:, :D_MODEL].astype(BF16)
            wy = rg_w_in[j][:, D_MODEL:].astype(BF16)
            b_in = rg_b_in[j].reshape(1, 2 * D_MODEL)
            cw = rg_conv_w[j]
            cb = rg_conv_b[j].reshape(1, D_MODEL)
            gw = rg_gate_w[j].astype(BF16)
            gbias = rg_gate_b[j]
            lam = rg_lambda[j]
            wo = rg_w_out[j].astype(BF16)
            bo = rg_b_out[j].reshape(1, D_MODEL)

            def mixer(xs, h0f, h0b, tm, is_ctx):
                hf, ab, bxb, y, hf_fin = _rg_a_call(xs, mod, g1, wu, wy, b_in, cw, cb, gw, gbias,
                                                    lam, h0f, tm=tm, is_ctx=is_ctx)
                out, hb_fin = _rg_b_call(xs, hf, ab, bxb, y, h0b, mod, wo, bo, tm=tm, is_ctx=is_ctx)
                return out, hf_fin, hb_fin

            ctx_mix, hf_c, hb_c = mixer(ctx, zero_state, zero_state, TM_CTX, True)
            x, _, _ = mixer(x, hf_c, hb_c, TM_LAT, False)
            x = _ffn_call(x, mod, g2, wgu, wd, None, tm=TM_LAT, is_ctx=False)
            ctx = _ffn_call(ctx_mix, mod, g2, wgu, wd, None, tm=TM_CTX, is_ctx=True)
        else:
            w_in = dn_w_in[j]
            wqkv = w_in[:, :DN_CONV_DIM].astype(BF16)
            wz = w_in[:, DN_CONV_DIM:DN_CONV_DIM + DN_VAL_DIM].astype(BF16)
            wba = _pad_lanes(w_in[:, DN_CONV_DIM + DN_VAL_DIM:]).astype(BF16)
            cw = dn_conv_w[j]
            n_bd = 2 * DN_V_HEADS
            alog = jnp.zeros((1, LANES), F32).at[0, n_bd:2 * n_bd].set(dn_a_log[j].reshape(-1))
            dtb = jnp.zeros((1, LANES), F32).at[0, n_bd:2 * n_bd].set(dn_dt_bias[j].reshape(-1))
            nw = dn_norm_w[j].reshape(1, DN_HEAD)
            wo = dn_w_out[j].astype(BF16)

            def prep(xs, tm, is_ctx):
                return _dn_prep_call(xs, mod, g1, wqkv, wz, wba, cw, alog, dtb, tm=tm, is_ctx=is_ctx)

            qc, kc, vc, zc, gbc = prep(ctx, TM_CTX, True)
            ofc, sf = _dn_core_call(qc, kc, vc, gbc, zero_s, reverse=False)
            obc, sb = _dn_core_call(qc, kc, vc, gbc, zero_s, reverse=True)
            x_cm = _to_column_major(x, n_rows)
            q, k, v, z, gb = prep(x_cm, TM_LAT, False)
            of, _ = _dn_core_call(q, k, v, gb, sf, reverse=False)
            ob, _ = _dn_core_call(q, k, v, gb, sb, reverse=True)
            x_cm = _dn_out_call(of, ob, z, x_cm, mod, nw, wo, tm=TM_LAT, is_ctx=False)
            x_cm = _ffn_call(x_cm, mod, g2, wgu, wd,
                             final_norm_g.reshape(1, D_MODEL) if last else None,
                             tm=TM_LAT, is_ctx=False)
            x = _to_row_major(x_cm, n_rows)
            if not last:
                ctx = _dn_out_call(ofc, obc, zc, ctx, mod, nw, wo, tm=TM_CTX, is_ctx=True)
                ctx = _ffn_call(ctx, mod, g2, wgu, wd, None, tm=TM_CTX, is_ctx=True)
    return x
```

```python
import functools

import jax
import jax.numpy as jnp
from jax import lax
from jax.experimental import pallas as pl
from jax.experimental.pallas import tpu as pltpu

F32 = jnp.float32
BF16 = jnp.bfloat16

D_MODEL = 1024
DEPTH = 4
GRID_W = 64
N_MOD = 6
NORM_EPS = 1e-6
CONV_W = 4
LRU_HEADS = 4
LRU_BLOCK = D_MODEL // LRU_HEADS
LRU_C = 8.0
DN_QK_HEADS = 8
DN_V_HEADS = 16
DN_HEAD = 128
DN_KEY_DIM = DN_QK_HEADS * DN_HEAD
DN_VAL_DIM = DN_V_HEADS * DN_HEAD
DN_CONV_DIM = 2 * DN_KEY_DIM + DN_VAL_DIM
D_FF = 2816

SUBLANES = 8
LANES = 128
HALO = SUBLANES
CHUNK = 128
DN_HEAD_GROUP = 16
MOD_ROWS = 8
CTX_MOD_ROW = 4
VMEM_LIMIT_BYTES = 56 * 1024 * 1024
FF_CHUNKS = ((0, 1536), (1536, 1280))


def _cparams(n_axes):
    return pltpu.CompilerParams(dimension_semantics=("arbitrary",) * n_axes,
                                vmem_limit_bytes=VMEM_LIMIT_BYTES)


def _whole(shape):
    nd = len(shape)
    return pl.BlockSpec(shape, lambda *_: (0,) * nd, pipeline_mode=pl.Buffered(1))


def _dot(a, b):
    return jnp.dot(a, b, preferred_element_type=F32)


def _silu(v):
    return v * jax.nn.sigmoid(v)


def _softplus(v):
    return jnp.maximum(v, 0.0) + jnp.log1p(jnp.exp(-jnp.abs(v)))


def _log_sigmoid(v):
    return jnp.minimum(v, 0.0) - jnp.log1p(jnp.exp(-jnp.abs(v)))


def _rmsnorm(x, g):
    return x * lax.rsqrt(jnp.mean(x * x, axis=-1, keepdims=True) + NORM_EPS) * g


def _norm_mod(x, g, shift, scale):
    return _rmsnorm(x, g) * (1.0 + scale) + shift


def _ada_kernel(act_ref, w_ref, b_ref, o_ref):
    a = _silu(act_ref[...])
    o_ref[0] = _dot(a, w_ref[0]) + b_ref[0]


def _ada_call(act, ada_w, ada_b):
    n_col = N_MOD * D_MODEL // D_MODEL
    return pl.pallas_call(
        _ada_kernel,
        grid=(DEPTH, n_col),
        in_specs=[pl.BlockSpec((MOD_ROWS, D_MODEL), lambda l, j: (0, 0)),
                  pl.BlockSpec((1, D_MODEL, D_MODEL), lambda l, j: (l, 0, j)),
                  pl.BlockSpec((1, 1, D_MODEL), lambda l, j: (l, 0, j))],
        out_specs=pl.BlockSpec((1, MOD_ROWS, D_MODEL), lambda l, j: (l, 0, j)),
        out_shape=jax.ShapeDtypeStruct((DEPTH, MOD_ROWS, N_MOD * D_MODEL), F32),
        compiler_params=_cparams(2),
        name="ada",
    )(act, ada_w, ada_b.reshape(DEPTH, 1, N_MOD * D_MODEL))


def _ffn_kernel(x_ref, mod_ref, g_ref, wgu_ref, wd_ref, *rest, final):
    if final:
        fg_ref, o_ref = rest
    else:
        (o_ref,) = rest
    x = x_ref[0]
    h = _norm_mod(x, g_ref[...], mod_ref[0, 3:4, :], mod_ref[0, 4:5, :]).astype(BF16)
    acc = None
    for c0, cw in FF_CHUNKS:
        gate = _dot(h, wgu_ref[:, c0:c0 + cw])
        up = _dot(h, wgu_ref[:, D_FF + c0:D_FF + c0 + cw])
        part = _dot((_silu(gate) * up).astype(BF16), wd_ref[c0:c0 + cw, :])
        acc = part if acc is None else acc + part
    y = x + mod_ref[0, 5:6, :] * acc
    if final:
        y = _rmsnorm(y, fg_ref[...])
    o_ref[0] = y


def _mod_spec(is_ctx):
    if is_ctx:
        return pl.BlockSpec((1, MOD_ROWS, D_MODEL), lambda b, i: (CTX_MOD_ROW, 0, 0))
    return pl.BlockSpec((1, MOD_ROWS, D_MODEL), lambda b, i: (b, 0, 0))


def _tok_spec(tm, width):
    return pl.BlockSpec((1, tm, width), lambda b, i: (b, i, 0))


def _ffn_call(x, mod, g, wgu, wd, final_g, *, tm, is_ctx):
    bsz, t_len, _ = x.shape
    final = final_g is not None
    in_specs = [_tok_spec(tm, D_MODEL), _mod_spec(is_ctx), _whole((1, D_MODEL)),
                _whole(wgu.shape), _whole(wd.shape)]
    args = [x, mod, g, wgu, wd]
    if final:
        in_specs.append(_whole((1, D_MODEL)))
        args.append(final_g)
    return pl.pallas_call(
        functools.partial(_ffn_kernel, final=final),
        grid=(bsz, t_len // tm),
        in_specs=in_specs,
        out_specs=_tok_spec(tm, D_MODEL),
        out_shape=jax.ShapeDtypeStruct(x.shape, F32),
        compiler_params=_cparams(2),
        name="ffn",
    )(*args)


def _halo_specs(tm, t_len):
    per = tm // HALO
    last = t_len // HALO - 1
    prev = pl.BlockSpec((1, HALO, D_MODEL), lambda b, i: (b, jnp.maximum(i * per - 1, 0), 0))
    nxt = pl.BlockSpec((1, HALO, D_MODEL), lambda b, i: (b, jnp.minimum((i + 1) * per, last), 0))
    return prev, nxt


def _load_normed_with_halo(xm_ref, xp_ref, xn_ref, xcat, g, shift, scale, tm):
    xcat[0:HALO, :] = xp_ref[0]
    xcat[HALO:HALO + tm, :] = xm_ref[0]
    xcat[HALO + tm:2 * HALO + tm, :] = xn_ref[0]
    return _norm_mod(xcat[...], g, shift, scale).astype(BF16)


def _halo_keep_mask(tm, i, nt):
    row = lax.broadcasted_iota(jnp.int32, (tm + 2 * HALO, 1), 0)
    keep_lo = jnp.logical_or(row >= HALO, i > 0)
    keep_hi = jnp.logical_or(row < HALO + tm, i < nt - 1)
    return jnp.logical_and(keep_lo, keep_hi)


def _dwconv_from(pbuf, cw, tm):
    out = None
    for j in range(CONV_W):
        term = cw[j:j + 1, :] * pbuf[pl.ds(HALO - 2 + j, tm), :]
        out = term if out is None else out + term
    return out


def _group_iota():
    return lax.broadcasted_iota(jnp.int32, (1, SUBLANES, 1), 1)


def _local_scan(a, b, tm, reverse):
    a3 = a.reshape(tm // SUBLANES, SUBLANES, D_MODEL)
    b3 = b.reshape(tm // SUBLANES, SUBLANES, D_MODEL)
    r = _group_iota()
    for s in (1, 2, 4):
        shift = SUBLANES - s if reverse else s
        a_sh = pltpu.roll(a3, shift, 1)
        b_sh = pltpu.roll(b3, shift, 1)
        valid = (r < SUBLANES - s) if reverse else (r >= s)
        b3 = jnp.where(valid, a3 * b_sh + b3, b3)
        a3 = jnp.where(valid, a3 * a_sh, a3)
    return a3.reshape(tm, D_MODEL), b3.reshape(tm, D_MODEL)


def _carry_scan(a_s, b_s, write, h, tm, reverse):
    n_grp = tm // SUBLANES

    def body(t, h):
        g = (n_grp - 1 - t) if reverse else t
        r0 = pl.multiple_of(g * SUBLANES, SUBLANES)
        hh = a_s[pl.ds(r0, SUBLANES), :] * h + b_s[pl.ds(r0, SUBLANES), :]
        write(r0, hh)
        return hh[0:1, :] if reverse else hh[SUBLANES - 1:SUBLANES, :]

    return lax.fori_loop(0, n_grp, body, h, unroll=8)


def _rg_a_kernel(xm_ref, xp_ref, xn_ref, mod_ref, g_ref, wu_ref, wy_ref, bin_ref, cw_ref, cb_ref,
                 gw_ref, gb_ref, lam_ref, h0_ref,
                 hf_ref, ab_ref, bxb_ref, y_ref, hfin_ref,
                 xcat, pbuf, a_s, b_s, carry, *, tm, nt):
    i = pl.program_id(1)
    h = _load_normed_with_halo(xm_ref, xp_ref, xn_ref, xcat, g_ref[...],
                               mod_ref[0, 0:1, :], mod_ref[0, 1:2, :], tm)
    pu = _dot(h, wu_ref[...]) + bin_ref[:, 0:D_MODEL]
    pbuf[...] = jnp.where(_halo_keep_mask(tm, i, nt), pu, 0.0)
    u = _dwconv_from(pbuf, cw_ref[...], tm) + cb_ref[...]
    py = _dot(h[HALO:HALO + tm, :], wy_ref[...]) + bin_ref[:, D_MODEL:2 * D_MODEL]
    y_ref[0] = jax.nn.gelu(py).astype(BF16)

    ub = u.astype(BF16)
    log_sig_lam = _log_sigmoid(lam_ref[...])
    for d in (0, 1):
        pre = []
        for gi in (0, 1):
            parts = [_dot(ub[:, hh * LRU_BLOCK:(hh + 1) * LRU_BLOCK], gw_ref[d, gi, hh])
                     for hh in range(LRU_HEADS)]
            pre.append(jnp.concatenate(parts, axis=1) + gb_ref[d, gi:gi + 1, :])
        r_gate = jax.nn.sigmoid(pre[0])
        i_gate = jax.nn.sigmoid(pre[1])
        log_a = LRU_C * r_gate * log_sig_lam[d:d + 1, :]
        a = jnp.exp(log_a)
        t = jnp.tanh(log_a)
        bx = jnp.sqrt(-2.0 * t / (1.0 - t)) * (i_gate * u)
        if d == 1:
            ab_ref[0] = a
            bxb_ref[0] = bx
        else:
            a_loc, b_loc = _local_scan(a, bx, tm, reverse=False)
            a_s[...] = a_loc
            b_s[...] = b_loc

    @pl.when(i == 0)
    def _():
        carry[...] = h0_ref[0]

    def write(r0, hh):
        hf_ref[0, pl.ds(r0, SUBLANES), :] = hh

    h_end = _carry_scan(a_s, b_s, write, carry[0:1, :], tm, reverse=False)
    carry[...] = jnp.broadcast_to(h_end, (SUBLANES, D_MODEL))

    @pl.when(i == nt - 1)
    def _():
        hfin_ref[0] = carry[...]


def _rg_a_call(x, mod, g, wu, wy, b_in, cw, cb, gw, gb, lam, h0, *, tm, is_ctx):
    bsz, t_len, _ = x.shape
    nt = t_len // tm
    prev, nxt = _halo_specs(tm, t_len)
    state_spec = pl.BlockSpec((1, SUBLANES, D_MODEL), lambda b, i: (b, 0, 0))
    tok = _tok_spec(tm, D_MODEL)
    return pl.pallas_call(
        functools.partial(_rg_a_kernel, tm=tm, nt=nt),
        grid=(bsz, nt),
        in_specs=[tok, prev, nxt, _mod_spec(is_ctx), _whole((1, D_MODEL)),
                  _whole(wu.shape), _whole(wy.shape), _whole(b_in.shape), _whole(cw.shape),
                  _whole(cb.shape), _whole(gw.shape), _whole(gb.shape), _whole(lam.shape),
                  state_spec],
        out_specs=[tok, tok, tok, tok, state_spec],
        out_shape=[jax.ShapeDtypeStruct(x.shape, F32), jax.ShapeDtypeStruct(x.shape, F32),
                   jax.ShapeDtypeStruct(x.shape, F32), jax.ShapeDtypeStruct(x.shape, BF16),
                   jax.ShapeDtypeStruct((bsz, SUBLANES, D_MODEL), F32)],
        scratch_shapes=[pltpu.VMEM((tm + 2 * HALO, D_MODEL), F32),
                        pltpu.VMEM((tm + 2 * HALO, D_MODEL), F32),
                        pltpu.VMEM((tm, D_MODEL), F32), pltpu.VMEM((tm, D_MODEL), F32),
                        pltpu.VMEM((SUBLANES, D_MODEL), F32)],
        compiler_params=_cparams(2),
        name="rg_a",
    )(x, x, x, mod, g, wu, wy, b_in, cw, cb, gw, gb, lam, h0)


def _rg_b_kernel(x_ref, hf_ref, ab_ref, bxb_ref, y_ref, h0_ref, mod_ref, wo_ref, bo_ref,
                 o_ref, hfin_ref, a_s, b_s, hb_s, carry, *, tm, nt):
    i = pl.program_id(1)
    a_loc, b_loc = _local_scan(ab_ref[0], bxb_ref[0], tm, reverse=True)
    a_s[...] = a_loc
    b_s[...] = b_loc

    @pl.when(i == 0)
    def _():
        carry[...] = h0_ref[0]

    def write(r0, hh):
        hb_s[pl.ds(r0, SUBLANES), :] = hh

    h_end = _carry_scan(a_s, b_s, write, carry[0:1, :], tm, reverse=True)
    carry[...] = jnp.broadcast_to(h_end, (SUBLANES, D_MODEL))

    @pl.when(i == nt - 1)
    def _():
        hfin_ref[0] = carry[...]

    hl = hf_ref[0] + hb_s[...]
    m = (hl * y_ref[0].astype(F32)).astype(BF16)
    out = _dot(m, wo_ref[...]) + bo_ref[...]
    o_ref[0] = x_ref[0] + mod_ref[0, 2:3, :] * out


def _rg_b_call(x, hf, ab, bxb, y, h0, mod, wo, bo, *, tm, is_ctx):
    bsz, t_len, _ = x.shape
    nt = t_len // tm
    tok = pl.BlockSpec((1, tm, D_MODEL), lambda b, i: (b, nt - 1 - i, 0))
    state_spec = pl.BlockSpec((1, SUBLANES, D_MODEL), lambda b, i: (b, 0, 0))
    return pl.pallas_call(
        functools.partial(_rg_b_kernel, tm=tm, nt=nt),
        grid=(bsz, nt),
        in_specs=[tok, tok, tok, tok, tok, state_spec, _mod_spec(is_ctx),
                  _whole(wo.shape), _whole(bo.shape)],
        out_specs=[tok, state_spec],
        out_shape=[jax.ShapeDtypeStruct(x.shape, F32),
                   jax.ShapeDtypeStruct((bsz, SUBLANES, D_MODEL), F32)],
        scratch_shapes=[pltpu.VMEM((tm, D_MODEL), F32), pltpu.VMEM((tm, D_MODEL), F32),
                        pltpu.VMEM((tm, D_MODEL), F32), pltpu.VMEM((SUBLANES, D_MODEL), F32)],
        compiler_params=_cparams(2),
        name="rg_b",
    )(x, hf, ab, bxb, y, h0, mod, wo, bo)


def _dn_prep_kernel(xm_ref, xp_ref, xn_ref, mod_ref, g_ref, wqkv_ref, wz_ref, wba_ref, cw_ref,
                    alog_ref, dtb_ref,
                    q_ref, k_ref, v_ref, z_ref, gb_ref,
                    xcat, pbuf, *, tm, nt):
    i = pl.program_id(1)
    h = _load_normed_with_halo(xm_ref, xp_ref, xn_ref, xcat, g_ref[...],
                               mod_ref[0, 0:1, :], mod_ref[0, 1:2, :], tm)
    keep = _halo_keep_mask(tm, i, nt)
    n_slab = DN_CONV_DIM // D_MODEL
    for c in range(n_slab):
        cols = slice(c * D_MODEL, (c + 1) * D_MODEL)
        pbuf[...] = jnp.where(keep, _dot(h, wqkv_ref[:, cols]), 0.0)
        u = _silu(_dwconv_from(pbuf, cw_ref[:, cols], tm))
        if c < 2:
            scale = DN_HEAD ** -0.5 if c == 0 else 1.0
            dst = q_ref if c == 0 else k_ref
            for hh in range(DN_QK_HEADS):
                uh = u[:, hh * DN_HEAD:(hh + 1) * DN_HEAD]
                uh = uh * lax.rsqrt(jnp.sum(uh * uh, axis=-1, keepdims=True) + NORM_EPS)
                if c == 0:
                    uh = uh * scale
                dst[0, :, hh * DN_HEAD:(hh + 1) * DN_HEAD] = uh.astype(BF16)
        else:
            v_ref[0, :, (c - 2) * D_MODEL:(c - 1) * D_MODEL] = u.astype(BF16)

    hm = h[HALO:HALO + tm, :]
    z_ref[0] = _dot(hm, wz_ref[...]).astype(BF16)
    ba = _dot(hm, wba_ref[...])
    lane = lax.broadcasted_iota(jnp.int32, (1, LANES), 1)
    beta = jax.nn.sigmoid(ba)
    gdec = -jnp.exp(alog_ref[...]) * _softplus(ba + dtb_ref[...])
    n_bd = 2 * DN_V_HEADS
    gb_ref[0] = jnp.where(lane < n_bd, beta, jnp.where(lane < 2 * n_bd, gdec, 0.0))


def _dn_prep_call(x, mod, g, wqkv, wz, wba, cw, alog, dtb, *, tm, is_ctx):
    bsz, t_len, _ = x.shape
    nt = t_len // tm
    prev, nxt = _halo_specs(tm, t_len)
    return pl.pallas_call(
        functools.partial(_dn_prep_kernel, tm=tm, nt=nt),
        grid=(bsz, nt),
        in_specs=[_tok_spec(tm, D_MODEL), prev, nxt, _mod_spec(is_ctx), _whole((1, D_MODEL)),
                  _whole(wqkv.shape), _whole(wz.shape), _whole(wba.shape), _whole(cw.shape),
                  _whole(alog.shape), _whole(dtb.shape)],
        out_specs=[_tok_spec(tm, DN_KEY_DIM), _tok_spec(tm, DN_KEY_DIM), _tok_spec(tm, DN_VAL_DIM),
                   _tok_spec(tm, DN_VAL_DIM), _tok_spec(tm, LANES)],
        out_shape=[jax.ShapeDtypeStruct((bsz, t_len, DN_KEY_DIM), BF16),
                   jax.ShapeDtypeStruct((bsz, t_len, DN_KEY_DIM), BF16),
                   jax.ShapeDtypeStruct((bsz, t_len, DN_VAL_DIM), BF16),
                   jax.ShapeDtypeStruct((bsz, t_len, DN_VAL_DIM), BF16),
                   jax.ShapeDtypeStruct((bsz, t_len, LANES), F32)],
        scratch_shapes=[pltpu.VMEM((tm + 2 * HALO, D_MODEL), F32),
                        pltpu.VMEM((tm + 2 * HALO, D_MODEL), F32)],
        compiler_params=_cparams(2),
        name="dn_prep",
    )(x, x, x, mod, g, wqkv, wz, wba, cw, alog, dtb)


def _level_masks(row, col):
    masks = []
    lvl = 1
    while (1 << lvl) < CHUNK:
        same = (row >> lvl) == (col >> lvl)
        same2 = (row >> (lvl + 1)) == (col >> (lvl + 1))
        masks.append(jnp.logical_and(same2, jnp.logical_not(same)))
        lvl += 1
    return masks


def _dn_core_kernel(q_ref, k_ref, v_ref, gb_ref, s0_ref, o_ref, sfin_ref, s_ref, *, nc, reverse):
    i = pl.program_id(1)

    @pl.when(i == 0)
    def _():
        s_ref[...] = s0_ref[0]

    row = lax.broadcasted_iota(jnp.int32, (CHUNK, CHUNK), 0)
    col = lax.broadcasted_iota(jnp.int32, (CHUNK, CHUNK), 1)
    incl = (row <= col) if reverse else (row >= col)
    strict = (row < col) if reverse else (row > col)
    base_blk = (row >> 1) == (col >> 1)
    eye = jnp.where(row == col, 1.0, 0.0)
    lvl_masks = _level_masks(row, col)
    gb = gb_ref[0]
    gc_all = jnp.dot(incl.astype(F32), gb, preferred_element_type=F32,
                     precision=lax.Precision.HIGHEST)
    gc_t = gc_all.T
    d = 1 if reverse else 0
    last = 0 if reverse else CHUNK - 1
    nt_dims = (((1,), (1,)), ((), ()))
    tn_dims = (((0,), (0,)), ((), ()))
    rep = DN_V_HEADS // DN_QK_HEADS

    def head_cols(hv):
        return slice(hv * DN_HEAD, (hv + 1) * DN_HEAD)

    for g0 in range(0, DN_V_HEADS, DN_HEAD_GROUP):
        heads = list(range(g0, g0 + DN_HEAD_GROUP))
        kk = {}
        qk = {}
        for hq in sorted({hv // rep for hv in heads}):
            qh = q_ref[0, :, head_cols(hq)]
            kh = k_ref[0, :, head_cols(hq)]
            kk[hq] = lax.dot_general(kh, kh, nt_dims, preferred_element_type=F32)
            qk[hq] = lax.dot_general(qh, kh, nt_dims, preferred_element_type=F32)
        beta_c, gc_c, mb, attn, x = {}, {}, {}, {}, {}
        for hv in heads:
            col_b = DN_V_HEADS * d + hv
            col_g = 2 * DN_V_HEADS + DN_V_HEADS * d + hv
            beta_c[hv] = gb[:, col_b:col_b + 1]
            gc_c[hv] = gc_all[:, col_g:col_g + 1]
            gc_r = gc_t[col_g:col_g + 1, :]
            decay = jnp.where(incl, jnp.exp(jnp.where(incl, gc_c[hv] - gc_r, 0.0)), 0.0)
            m = jnp.where(strict, kk[hv // rep] * beta_c[hv] * decay, 0.0)
            x[hv] = eye - jnp.where(base_blk, m, 0.0)
            mb[hv] = m.astype(BF16)
            attn[hv] = jnp.where(incl, qk[hv // rep] * decay, 0.0).astype(BF16)
        for mask in lvl_masks:
            xb = {hv: x[hv].astype(BF16) for hv in heads}
            t = {hv: _dot(xb[hv], jnp.where(mask, mb[hv], 0.0).astype(BF16)).astype(BF16)
                 for hv in heads}
            x = {hv: x[hv] - _dot(t[hv], xb[hv]) for hv in heads}
        sb, resid = {}, {}
        for hv in heads:
            khf = k_ref[0, :, head_cols(hv // rep)].astype(F32)
            kg = (khf * beta_c[hv] * jnp.exp(gc_c[hv])).astype(BF16)
            vb = v_ref[0, :, head_cols(hv)].astype(F32) * beta_c[hv]
            sb[hv] = s_ref[hv].astype(BF16)
            resid[hv] = (vb - _dot(kg, sb[hv])).astype(BF16)
        v_new = {hv: _dot(x[hv].astype(BF16), resid[hv]).astype(BF16) for hv in heads}
        for hv in heads:
            qg = (q_ref[0, :, head_cols(hv // rep)].astype(F32) * jnp.exp(gc_c[hv])).astype(BF16)
            o_ref[0, :, head_cols(hv)] = _dot(qg, sb[hv]) + _dot(attn[hv], v_new[hv])
        for hv in heads:
            g_last = gc_c[hv][last:last + 1, :]
            khf = k_ref[0, :, head_cols(hv // rep)].astype(F32)
            kd = (khf * jnp.exp(g_last - gc_c[hv])).astype(BF16)
            s_ref[hv] = s_ref[hv] * jnp.exp(g_last) + lax.dot_general(
                kd, v_new[hv], tn_dims, preferred_element_type=F32)

    @pl.when(i == nc - 1)
    def _():
        sfin_ref[0] = s_ref[...]


def _dn_core_call(q, k, v, gb, s0, *, reverse):
    bsz, t_len, _ = q.shape
    nc = t_len // CHUNK
    if reverse:
        imap = lambda b, i: (b, nc - 1 - i, 0)
    else:
        imap = lambda b, i: (b, i, 0)
    state_spec = pl.BlockSpec((1, DN_V_HEADS, DN_HEAD, DN_HEAD), lambda b, i: (b, 0, 0, 0))
    return pl.pallas_call(
        functools.partial(_dn_core_kernel, nc=nc, reverse=reverse),
        grid=(bsz, nc),
        in_specs=[pl.BlockSpec((1, CHUNK, DN_KEY_DIM), imap), pl.BlockSpec((1, CHUNK, DN_KEY_DIM), imap),
                  pl.BlockSpec((1, CHUNK, DN_VAL_DIM), imap), pl.BlockSpec((1, CHUNK, LANES), imap),
                  state_spec],
        out_specs=[pl.BlockSpec((1, CHUNK, DN_VAL_DIM), imap), state_spec],
        out_shape=[jax.ShapeDtypeStruct((bsz, t_len, DN_VAL_DIM), F32),
                   jax.ShapeDtypeStruct((bsz, DN_V_HEADS, DN_HEAD, DN_HEAD), F32)],
        scratch_shapes=[pltpu.VMEM((DN_V_HEADS, DN_HEAD, DN_HEAD), F32)],
        compiler_params=_cparams(2),
        name="dn_core_bwd" if reverse else "dn_core_fwd",
    )(q, k, v, gb, s0)


def _dn_out_kernel(of_ref, ob_ref, z_ref, x_ref, mod_ref, nw_ref, wo_ref, o_ref):
    o = of_ref[0] + ob_ref[0]
    parts = []
    for hh in range(DN_V_HEADS):
        cols = slice(hh * DN_HEAD, (hh + 1) * DN_HEAD)
        oh = o[:, cols]
        oh = oh * lax.rsqrt(jnp.mean(oh * oh, axis=-1, keepdims=True) + NORM_EPS) * nw_ref[...]
        parts.append((oh * _silu(z_ref[0, :, cols].astype(F32))).astype(BF16))
    y = _dot(jnp.concatenate(parts, axis=1), wo_ref[...])
    o_ref[0] = x_ref[0] + mod_ref[0, 2:3, :] * y


def _dn_out_call(of, ob, z, x, mod, nw, wo, *, tm, is_ctx):
    bsz, t_len, _ = x.shape
    return pl.pallas_call(
        _dn_out_kernel,
        grid=(bsz, t_len // tm),
        in_specs=[_tok_spec(tm, DN_VAL_DIM), _tok_spec(tm, DN_VAL_DIM), _tok_spec(tm, DN_VAL_DIM),
                  _tok_spec(tm, D_MODEL), _mod_spec(is_ctx), _whole(nw.shape), _whole(wo.shape)],
        out_specs=_tok_spec(tm, D_MODEL),
        out_shape=jax.ShapeDtypeStruct(x.shape, F32),
        compiler_params=_cparams(2),
        name="dn_out",
    )(of, ob, z, x, mod, nw, wo)


def _to_column_major(h, n_rows):
    b, _, d = h.shape
    return h.reshape(b, n_rows, GRID_W, d).transpose(0, 2, 1, 3).reshape(b, -1, d)


def _to_row_major(h, n_rows):
    b, _, d = h.shape
    return h.reshape(b, GRID_W, n_rows, d).transpose(0, 2, 1, 3).reshape(b, -1, d)


def _pad_lanes(a, width=LANES):
    return jnp.pad(a, [(0, 0)] * (a.ndim - 1) + [(0, width - a.shape[-1])])


TM_LAT = 512
TM_CTX = 256


def kernel(x, c, ctx, c_ctx, ada_w, ada_b, norm_g, final_norm_g, rg_w_in, rg_b_in, rg_conv_w,
           rg_conv_b, rg_gate_w, rg_gate_b, rg_lambda, rg_w_out, rg_b_out, dn_w_in, dn_conv_w,
           dn_a_log, dn_dt_bias, dn_norm_w, dn_w_out, ffn_w_gu, ffn_w_down):
    bsz, t_len, _ = x.shape
    n_rows = t_len // GRID_W
    act = jnp.zeros((MOD_ROWS, D_MODEL), F32).at[0:bsz].set(c).at[CTX_MOD_ROW].set(c_ctx)
    mods = _ada_call(act, ada_w, ada_b).reshape(DEPTH, MOD_ROWS, N_MOD, D_MODEL)
    mods = jnp.pad(mods, ((0, 0), (0, 0), (0, MOD_ROWS - N_MOD), (0, 0)))
    zero_state = jnp.zeros((bsz, SUBLANES, D_MODEL), F32)
    zero_s = jnp.zeros((bsz, DN_V_HEADS, DN_HEAD, DN_HEAD), F32)

    for layer in range(DEPTH):
        last = layer == DEPTH - 1
        mod = mods[layer]
        g1 = norm_g[layer, 0].reshape(1, D_MODEL)
        g2 = norm_g[layer, 1].reshape(1, D_MODEL)
        wgu = ffn_w_gu[layer].astype(BF16)
        wd = ffn_w_down[layer].astype(BF16)
        j = layer // 2
        if layer % 2 == 0:
            wu = rg_w_in[j][:, :D_MODEL].astype(BF16)
            wy = rg_w_in[j][:, D_MODEL:].astype(BF16)
            b_in = rg_b_in[j].reshape(1, 2 * D_MODEL)
            cw = rg_conv_w[j]
            cb = rg_conv_b[j].reshape(1, D_MODEL)
            gw = rg_gate_w[j].astype(BF16)
            gbias = rg_gate_b[j]
            lam = rg_lambda[j]
            wo = rg_w_out[j].astype(BF16)
            bo = rg_b_out[j].reshape(1, D_MODEL)

            def mixer(xs, h0f, h0b, tm, is_ctx):
                hf, ab, bxb, y, hf_fin = _rg_a_call(xs, mod, g1, wu, wy, b_in, cw, cb, gw, gbias,
                                                    lam, h0f, tm=tm, is_ctx=is_ctx)
                out, hb_fin = _rg_b_call(xs, hf, ab, bxb, y, h0b, mod, wo, bo, tm=tm, is_ctx=is_ctx)
                return out, hf_fin, hb_fin

            ctx_mix, hf_c, hb_c = mixer(ctx, zero_state, zero_state, TM_CTX, True)
            x, _, _ = mixer(x, hf_c, hb_c, TM_LAT, False)
            x = _ffn_call(x, mod, g2, wgu, wd, None, tm=TM_LAT, is_ctx=False)
            ctx = _ffn_call(ctx_mix, mod, g2, wgu, wd, None, tm=TM_CTX, is_ctx=True)
        else:
            w_in = dn_w_in[j]
            wqkv = w_in[:, :DN_CONV_DIM].astype(BF16)
            wz = w_in[:, DN_CONV_DIM:DN_CONV_DIM + DN_VAL_DIM].astype(BF16)
            wba = _pad_lanes(w_in[:, DN_CONV_DIM + DN_VAL_DIM:]).astype(BF16)
            cw = dn_conv_w[j]
            n_bd = 2 * DN_V_HEADS
            alog = jnp.zeros((1, LANES), F32).at[0, n_bd:2 * n_bd].set(dn_a_log[j].reshape(-1))
            dtb = jnp.zeros((1, LANES), F32).at[0, n_bd:2 * n_bd].set(dn_dt_bias[j].reshape(-1))
            nw = dn_norm_w[j].reshape(1, DN_HEAD)
            wo = dn_w_out[j].astype(BF16)

            def prep(xs, tm, is_ctx):
                return _dn_prep_call(xs, mod, g1, wqkv, wz, wba, cw, alog, dtb, tm=tm, is_ctx=is_ctx)

            qc, kc, vc, zc, gbc = prep(ctx, TM_CTX, True)
            ofc, sf = _dn_core_call(qc, kc, vc, gbc, zero_s, reverse=False)
            obc, sb = _dn_core_call(qc, kc, vc, gbc, zero_s, reverse=True)
            x_cm = _to_column_major(x, n_rows)
            q, k, v, z, gb = prep(x_cm, TM_LAT, False)
            of, _ = _dn_core_call(q, k, v, gb, sf, reverse=False)
            ob, _ = _dn_core_call(q, k, v, gb, sb, reverse=True)
            x_cm = _dn_out_call(of, ob, z, x_cm, mod, nw, wo, tm=TM_LAT, is_ctx=False)
            x_cm = _ffn_call(x_cm, mod, g2, wgu, wd,
                             final_norm_g.reshape(1, D_MODEL) if last else None,
                             tm=TM_LAT, is_ctx=False)
            x = _to_row_major(x_cm, n_rows)
            if not last:
                ctx = _dn_out_call(ofc, obc, zc, ctx, mod, nw, wo, tm=TM_CTX, is_ctx=True)
                ctx = _ffn_call(ctx, mod, g2, wgu, wd, None, tm=TM_CTX, is_ctx=True)
    return x
```

```python
import functools

import jax
import jax.numpy as jnp
from jax import lax
from jax.experimental import pallas as pl
from jax.experimental.pallas import tpu as pltpu

F32 = jnp.float32
BF16 = jnp.bfloat16

D_MODEL = 1024
DEPTH = 4
GRID_W = 64
N_MOD = 6
NORM_EPS = 1e-6
CONV_W = 4
LRU_HEADS = 4
LRU_BLOCK = D_MODEL // LRU_HEADS
LRU_C = 8.0
DN_QK_HEADS = 8
DN_V_HEADS = 16
DN_HEAD = 128
DN_KEY_DIM = DN_QK_HEADS * DN_HEAD
DN_VAL_DIM = DN_V_HEADS * DN_HEAD
DN_CONV_DIM = 2 * DN_KEY_DIM + DN_VAL_DIM
D_FF = 2816

SUBLANES = 8
LANES = 128
HALO = SUBLANES
CHUNK = 128
DN_INST_GROUP = 16
MOD_ROWS = 8
CTX_MOD_ROW = 4
VMEM_LIMIT_BYTES = 56 * 1024 * 1024
FF_CHUNKS = ((0, 1536), (1536, 1280))
TM_PREP = 512
TM_POST = 256
TM_CTX = 256


def _cparams(n_axes):
    return pltpu.CompilerParams(dimension_semantics=("arbitrary",) * n_axes,
                                vmem_limit_bytes=VMEM_LIMIT_BYTES)


def _whole(shape):
    nd = len(shape)
    return pl.BlockSpec(shape, lambda *_: (0,) * nd, pipeline_mode=pl.Buffered(1))


def _dot(a, b):
    return jnp.dot(a, b, preferred_element_type=F32)


def _sigmoid(v):
    return 0.5 + 0.5 * jnp.tanh(0.5 * v)


def _silu(v):
    hv = 0.5 * v
    return hv + hv * jnp.tanh(hv)


def _softplus(v):
    return jnp.maximum(v, 0.0) + jnp.log1p(jnp.exp(-jnp.abs(v)))


def _log_sigmoid(v):
    return jnp.minimum(v, 0.0) - jnp.log1p(jnp.exp(-jnp.abs(v)))


def _rmsnorm(x, g):
    return x * lax.rsqrt(jnp.mean(x * x, axis=-1, keepdims=True) + NORM_EPS) * g


def _norm_mod(x, g, shift, scale):
    return _rmsnorm(x, g) * (1.0 + scale) + shift


def _mod_spec(is_ctx):
    if is_ctx:
        return pl.BlockSpec((1, MOD_ROWS, D_MODEL), lambda b, i: (CTX_MOD_ROW, 0, 0))
    return pl.BlockSpec((1, MOD_ROWS, D_MODEL), lambda b, i: (b, 0, 0))


def _tok_spec(tm, width):
    return pl.BlockSpec((1, tm, width), lambda b, i: (b, i, 0))


def _ada_kernel(act_ref, w_ref, b_ref, o_ref):
    a = _silu(act_ref[...])
    o_ref[0] = _dot(a, w_ref[0]) + b_ref[0]


def _ada_call(act, ada_w, ada_b):
    return pl.pallas_call(
        _ada_kernel,
        grid=(DEPTH, N_MOD),
        in_specs=[pl.BlockSpec((MOD_ROWS, D_MODEL), lambda l, j: (0, 0)),
                  pl.BlockSpec((1, D_MODEL, D_MODEL), lambda l, j: (l, 0, j)),
                  pl.BlockSpec((1, 1, D_MODEL), lambda l, j: (l, 0, j))],
        out_specs=pl.BlockSpec((1, MOD_ROWS, D_MODEL), lambda l, j: (l, 0, j)),
        out_shape=jax.ShapeDtypeStruct((DEPTH, MOD_ROWS, N_MOD * D_MODEL), F32),
        compiler_params=_cparams(2),
        name="ada",
    )(act, ada_w, ada_b.reshape(DEPTH, 1, N_MOD * D_MODEL))


def _ffn_residual(x, mod_ref, g_ref, wgu_ref, wd_ref, fg_ref):
    h = _norm_mod(x, g_ref[...], mod_ref[0, 3:4, :], mod_ref[0, 4:5, :]).astype(BF16)
    acc = None
    for c0, cw in FF_CHUNKS:
        gate = _dot(h, wgu_ref[:, c0:c0 + cw])
        up = _dot(h, wgu_ref[:, D_FF + c0:D_FF + c0 + cw])
        part = _dot((_silu(gate) * up).astype(BF16), wd_ref[c0:c0 + cw, :])
        acc = part if acc is None else acc + part
    y = x + mod_ref[0, 5:6, :] * acc
    if fg_ref is not None:
        y = _rmsnorm(y, fg_ref[...])
    return y


def _halo_specs(tm, t_len):
    per = tm // HALO
    last = t_len // HALO - 1
    prev = pl.BlockSpec((1, HALO, D_MODEL), lambda b, i: (b, jnp.maximum(i * per - 1, 0), 0))
    nxt = pl.BlockSpec((1, HALO, D_MODEL), lambda b, i: (b, jnp.minimum((i + 1) * per, last), 0))
    return prev, nxt


def _load_normed_with_halo(xm_ref, xp_ref, xn_ref, xcat, g, shift, scale, tm):
    xcat[0:HALO, :] = xp_ref[0]
    xcat[HALO:HALO + tm, :] = xm_ref[0]
    xcat[HALO + tm:2 * HALO + tm, :] = xn_ref[0]
    return _norm_mod(xcat[...], g, shift, scale).astype(BF16)


def _group_iota():
    return lax.broadcasted_iota(jnp.int32, (1, SUBLANES, 1), 1)


def _dwconv(p, cw, i, nt, tm):
    n_grp = tm // SUBLANES
    c = p.shape[-1]
    p3 = p.reshape(n_grp + 2, SUBLANES, c)
    lo = p3[0:1] * jnp.where(i > 0, 1.0, 0.0)
    hi = p3[n_grp + 1:n_grp + 2] * jnp.where(i < nt - 1, 1.0, 0.0)
    p3 = jnp.concatenate([lo, p3[1:n_grp + 1], hi], axis=0)
    r = _group_iota()
    r2 = pltpu.roll(p3, 2, 1)
    r1 = pltpu.roll(p3, 1, 1)
    r7 = pltpu.roll(p3, SUBLANES - 1, 1)
    tap_m2 = jnp.where(r >= 2, r2[1:n_grp + 1], r2[0:n_grp])
    tap_m1 = jnp.where(r >= 1, r1[1:n_grp + 1], r1[0:n_grp])
    tap_p1 = jnp.where(r < SUBLANES - 1, r7[1:n_grp + 1], r7[2:n_grp + 2])
    w = [cw[j:j + 1, :].reshape(1, 1, c) for j in range(CONV_W)]
    out = w[0] * tap_m2 + w[1] * tap_m1 + w[2] * p3[1:n_grp + 1] + w[3] * tap_p1
    return out.reshape(tm, c)


def _local_scan(a, b, tm, reverse):
    a3 = a.reshape(tm // SUBLANES, SUBLANES, D_MODEL)
    b3 = b.reshape(tm // SUBLANES, SUBLANES, D_MODEL)
    r = _group_iota()
    for s in (1, 2, 4):
        shift = SUBLANES - s if reverse else s
        a_sh = pltpu.roll(a3, shift, 1)
        b_sh = pltpu.roll(b3, shift, 1)
        valid = (r < SUBLANES - s) if reverse else (r >= s)
        b3 = jnp.where(valid, a3 * b_sh + b3, b3)
        a3 = jnp.where(valid, a3 * a_sh, a3)
    return a3.reshape(tm, D_MODEL), b3.reshape(tm, D_MODEL)


def _carry_scan(a_s, b_s, write, h, tm, reverse):
    n_grp = tm // SUBLANES

    def body(t, h):
        g = (n_grp - 1 - t) if reverse else t
        r0 = pl.multiple_of(g * SUBLANES, SUBLANES)
        hh = a_s[pl.ds(r0, SUBLANES), :] * h + b_s[pl.ds(r0, SUBLANES), :]
        write(r0, hh)
        return hh[0:1, :] if reverse else hh[SUBLANES - 1:SUBLANES, :]

    return lax.fori_loop(0, n_grp, body, h, unroll=8)


def _rg_a_kernel(xm_ref, xp_ref, xn_ref, mod_ref, g_ref, wu_ref, wy_ref, bin_ref, cw_ref, cb_ref,
                 gw_ref, gb_ref, lam_ref, h0_ref,
                 hf_ref, ab_ref, bxb_ref, y_ref, hfin_ref,
                 xcat, a_s, b_s, carry, *, tm, nt):
    i = pl.program_id(1)
    h = _load_normed_with_halo(xm_ref, xp_ref, xn_ref, xcat, g_ref[...],
                               mod_ref[0, 0:1, :], mod_ref[0, 1:2, :], tm)
    pu = _dot(h, wu_ref[...]) + bin_ref[:, 0:D_MODEL]
    u = _dwconv(pu, cw_ref[...], i, nt, tm) + cb_ref[...]
    py = _dot(h[HALO:HALO + tm, :], wy_ref[...]) + bin_ref[:, D_MODEL:2 * D_MODEL]
    y_ref[0] = jax.nn.gelu(py).astype(BF16)

    ub = u.astype(BF16)
    log_sig_lam = _log_sigmoid(lam_ref[...])
    for d in (0, 1):
        pre = []
        for gi in (0, 1):
            parts = [_dot(ub[:, hh * LRU_BLOCK:(hh + 1) * LRU_BLOCK], gw_ref[d, gi, hh])
                     for hh in range(LRU_HEADS)]
            pre.append(jnp.concatenate(parts, axis=1) + gb_ref[d, gi:gi + 1, :])
        r_gate = _sigmoid(pre[0])
        i_gate = _sigmoid(pre[1])
        log_a = LRU_C * r_gate * log_sig_lam[d:d + 1, :]
        a = jnp.exp(log_a)
        t = jnp.tanh(log_a)
        bx = jnp.sqrt(-2.0 * t / (1.0 - t)) * (i_gate * u)
        if d == 1:
            ab_ref[0] = a
            bxb_ref[0] = bx
        else:
            a_loc, b_loc = _local_scan(a, bx, tm, reverse=False)
            a_s[...] = a_loc
            b_s[...] = b_loc

    @pl.when(i == 0)
    def _():
        carry[...] = h0_ref[0]

    def write(r0, hh):
        hf_ref[0, pl.ds(r0, SUBLANES), :] = hh

    h_end = _carry_scan(a_s, b_s, write, carry[0:1, :], tm, reverse=False)
    carry[...] = jnp.broadcast_to(h_end, (SUBLANES, D_MODEL))

    @pl.when(i == nt - 1)
    def _():
        hfin_ref[0] = carry[...]


def _rg_a_call(x, mod, g, wu, wy, b_in, cw, cb, gw, gb, lam, h0, *, tm, is_ctx):
    bsz, t_len, _ = x.shape
    nt = t_len // tm
    prev, nxt = _halo_specs(tm, t_len)
    state_spec = pl.BlockSpec((1, SUBLANES, D_MODEL), lambda b, i: (b, 0, 0))
    tok = _tok_spec(tm, D_MODEL)
    return pl.pallas_call(
        functools.partial(_rg_a_kernel, tm=tm, nt=nt),
        grid=(bsz, nt),
        in_specs=[tok, prev, nxt, _mod_spec(is_ctx), _whole((1, D_MODEL)),
                  _whole(wu.shape), _whole(wy.shape), _whole(b_in.shape), _whole(cw.shape),
                  _whole(cb.shape), _whole(gw.shape), _whole(gb.shape), _whole(lam.shape),
                  state_spec],
        out_specs=[tok, tok, tok, tok, state_spec],
        out_shape=[jax.ShapeDtypeStruct(x.shape, F32), jax.ShapeDtypeStruct(x.shape, F32),
                   jax.ShapeDtypeStruct(x.shape, F32), jax.ShapeDtypeStruct(x.shape, BF16),
                   jax.ShapeDtypeStruct((bsz, SUBLANES, D_MODEL), F32)],
        scratch_shapes=[pltpu.VMEM((tm + 2 * HALO, D_MODEL), F32),
                        pltpu.VMEM((tm, D_MODEL), F32), pltpu.VMEM((tm, D_MODEL), F32),
                        pltpu.VMEM((SUBLANES, D_MODEL), F32)],
        compiler_params=_cparams(2),
        name="rg_a",
    )(x, x, x, mod, g, wu, wy, b_in, cw, cb, gw, gb, lam, h0)


def _rg_post_kernel(x_ref, hf_ref, ab_ref, bxb_ref, y_ref, h0_ref, mod_ref, wo_ref, bo_ref,
                    g2_ref, wgu_ref, wd_ref, o_ref, hfin_ref, a_s, b_s, carry, *, tm, nt):
    i = pl.program_id(1)
    a_loc, b_loc = _local_scan(ab_ref[0], bxb_ref[0], tm, reverse=True)
    a_s[...] = a_loc
    b_s[...] = b_loc

    @pl.when(i == 0)
    def _():
        carry[...] = h0_ref[0]

    def write(r0, hh):
        b_s[pl.ds(r0, SUBLANES), :] = hh

    h_end = _carry_scan(a_s, b_s, write, carry[0:1, :], tm, reverse=True)
    carry[...] = jnp.broadcast_to(h_end, (SUBLANES, D_MODEL))

    @pl.when(i == nt - 1)
    def _():
        hfin_ref[0] = carry[...]

    hl = hf_ref[0] + b_s[...]
    m = (hl * y_ref[0].astype(F32)).astype(BF16)
    out = _dot(m, wo_ref[...]) + bo_ref[...]
    x1 = x_ref[0] + mod_ref[0, 2:3, :] * out
    o_ref[0] = _ffn_residual(x1, mod_ref, g2_ref, wgu_ref, wd_ref, None)


def _rg_post_call(x, hf, ab, bxb, y, h0, mod, wo, bo, g2, wgu, wd, *, tm, is_ctx):
    bsz, t_len, _ = x.shape
    nt = t_len // tm
    tok = pl.BlockSpec((1, tm, D_MODEL), lambda b, i: (b, nt - 1 - i, 0))
    state_spec = pl.BlockSpec((1, SUBLANES, D_MODEL), lambda b, i: (b, 0, 0))
    return pl.pallas_call(
        functools.partial(_rg_post_kernel, tm=tm, nt=nt),
        grid=(bsz, nt),
        in_specs=[tok, tok, tok, tok, tok, state_spec, _mod_spec(is_ctx),
                  _whole(wo.shape), _whole(bo.shape), _whole((1, D_MODEL)),
                  _whole(wgu.shape), _whole(wd.shape)],
        out_specs=[tok, state_spec],
        out_shape=[jax.ShapeDtypeStruct(x.shape, F32),
                   jax.ShapeDtypeStruct((bsz, SUBLANES, D_MODEL), F32)],
        scratch_shapes=[pltpu.VMEM((tm, D_MODEL), F32), pltpu.VMEM((tm, D_MODEL), F32),
                        pltpu.VMEM((SUBLANES, D_MODEL), F32)],
        compiler_params=_cparams(2),
        name="rg_post",
    )(x, hf, ab, bxb, y, h0, mod, wo, bo, g2, wgu, wd)


def _dn_prep_kernel(xm_ref, xp_ref, xn_ref, mod_ref, g_ref, wqkv_ref, wz_ref, wba_ref, cw_ref,
                    alog_ref, dtb_ref,
                    q_ref, k_ref, v_ref, z_ref, gb_ref,
                    xcat, *, tm, nt):
    i = pl.program_id(1)
    h = _load_normed_with_halo(xm_ref, xp_ref, xn_ref, xcat, g_ref[...],
                               mod_ref[0, 0:1, :], mod_ref[0, 1:2, :], tm)
    n_slab = DN_CONV_DIM // D_MODEL
    for c in range(n_slab):
        cols = slice(c * D_MODEL, (c + 1) * D_MODEL)
        u = _silu(_dwconv(_dot(h, wqkv_ref[:, cols]), cw_ref[:, cols], i, nt, tm))
        if c < 2:
            dst = q_ref if c == 0 else k_ref
            for hh in range(DN_QK_HEADS):
                uh = u[:, hh * DN_HEAD:(hh + 1) * DN_HEAD]
                uh = uh * lax.rsqrt(jnp.sum(uh * uh, axis=-1, keepdims=True) + NORM_EPS)
                if c == 0:
                    uh = uh * DN_HEAD ** -0.5
                dst[0, :, hh * DN_HEAD:(hh + 1) * DN_HEAD] = uh.astype(BF16)
        else:
            v_ref[0, :, (c - 2) * D_MODEL:(c - 1) * D_MODEL] = u.astype(BF16)

    hm = h[HALO:HALO + tm, :]
    z_ref[0] = _dot(hm, wz_ref[...]).astype(BF16)
    ba = _dot(hm, wba_ref[...])
    lane = lax.broadcasted_iota(jnp.int32, (1, LANES), 1)
    beta = _sigmoid(ba)
    gdec = -jnp.exp(alog_ref[...]) * _softplus(ba + dtb_ref[...])
    n_bd = 2 * DN_V_HEADS
    gb_ref[0] = jnp.where(lane < n_bd, beta, jnp.where(lane < 2 * n_bd, gdec, 0.0))


def _dn_prep_call(x, mod, g, wqkv, wz, wba, cw, alog, dtb, *, tm, is_ctx):
    bsz, t_len, _ = x.shape
    nt = t_len // tm
    prev, nxt = _halo_specs(tm, t_len)
    return pl.pallas_call(
        functools.partial(_dn_prep_kernel, tm=tm, nt=nt),
        grid=(bsz, nt),
        in_specs=[_tok_spec(tm, D_MODEL), prev, nxt, _mod_spec(is_ctx), _whole((1, D_MODEL)),
                  _whole(wqkv.shape), _whole(wz.shape), _whole(wba.shape), _whole(cw.shape),
                  _whole(alog.shape), _whole(dtb.shape)],
        out_specs=[_tok_spec(tm, DN_KEY_DIM), _tok_spec(tm, DN_KEY_DIM), _tok_spec(tm, DN_VAL_DIM),
                   _tok_spec(tm, DN_VAL_DIM), _tok_spec(tm, LANES)],
        out_shape=[jax.ShapeDtypeStruct((bsz, t_len, DN_KEY_DIM), BF16),
                   jax.ShapeDtypeStruct((bsz, t_len, DN_KEY_DIM), BF16),
                   jax.ShapeDtypeStruct((bsz, t_len, DN_VAL_DIM), BF16),
                   jax.ShapeDtypeStruct((bsz, t_len, DN_VAL_DIM), BF16),
                   jax.ShapeDtypeStruct((bsz, t_len, LANES), F32)],
        scratch_shapes=[pltpu.VMEM((tm + 2 * HALO, D_MODEL), F32)],
        compiler_params=_cparams(2),
        name="dn_prep",
    )(x, x, x, mod, g, wqkv, wz, wba, cw, alog, dtb)


def _level_masks(row, col):
    masks = []
    lvl = 1
    while (1 << lvl) < CHUNK:
        same = (row >> lvl) == (col >> lvl)
        same2 = (row >> (lvl + 1)) == (col >> (lvl + 1))
        masks.append(jnp.logical_and(same2, jnp.logical_not(same)))
        lvl += 1
    return masks


def _dn_core_kernel(qf_ref, kf_ref, vf_ref, gbf_ref, qb_ref, kb_ref, vb_ref, gbb_ref,
                    s0f_ref, s0b_ref, of_ref, ob_ref, sfinf_ref, sfinb_ref, sf_s, sb_s, *, nc):
    i = pl.program_id(1)

    @pl.when(i == 0)
    def _():
        sf_s[...] = s0f_ref[0]
        sb_s[...] = s0b_ref[0]

    row = lax.broadcasted_iota(jnp.int32, (CHUNK, CHUNK), 0)
    col = lax.broadcasted_iota(jnp.int32, (CHUNK, CHUNK), 1)
    base_blk = (row >> 1) == (col >> 1)
    eye = jnp.where(row == col, 1.0, 0.0)
    lvl_masks = _level_masks(row, col)
    nt_dims = (((1,), (1,)), ((), ()))
    tn_dims = (((0,), (0,)), ((), ()))
    rep = DN_V_HEADS // DN_QK_HEADS

    def head_cols(hv):
        return slice(hv * DN_HEAD, (hv + 1) * DN_HEAD)

    dirs = []
    for d, (q_ref, k_ref, v_ref, gb_ref, o_ref, s_ref) in enumerate(
            ((qf_ref, kf_ref, vf_ref, gbf_ref, of_ref, sf_s),
             (qb_ref, kb_ref, vb_ref, gbb_ref, ob_ref, sb_s))):
        incl = (row <= col) if d == 1 else (row >= col)
        strict = (row < col) if d == 1 else (row > col)
        gb = gb_ref[0]
        gc_all = jnp.dot(incl.astype(F32), gb, preferred_element_type=F32,
                         precision=lax.Precision.HIGHEST)
        dirs.append(dict(q=q_ref, k=k_ref, v=v_ref, o=o_ref, s=s_ref, incl=incl, strict=strict,
                         gb=gb, gc=gc_all, gc_t=gc_all.T, last=0 if d == 1 else CHUNK - 1))

    insts = [(d, hv) for d in (0, 1) for hv in range(DN_V_HEADS)]
    for g0 in range(0, len(insts), DN_INST_GROUP):
        grp = insts[g0:g0 + DN_INST_GROUP]
        kk, qk = {}, {}
        for d, hq in sorted({(d, hv // rep) for d, hv in grp}):
            kh = dirs[d]["k"][0, :, head_cols(hq)]
            qh = dirs[d]["q"][0, :, head_cols(hq)]
            both = lax.dot_general(jnp.concatenate([kh, qh], axis=0), kh, nt_dims,
                                   preferred_element_type=F32)
            kk[d, hq] = both[0:CHUNK]
            qk[d, hq] = both[CHUNK:2 * CHUNK]
        beta_c, gc_c, mb, attn, x = {}, {}, {}, {}, {}
        for key in grp:
            d, hv = key
            dd = dirs[d]
            col_b = DN_V_HEADS * d + hv
            col_g = 2 * DN_V_HEADS + DN_V_HEADS * d + hv
            beta_c[key] = dd["gb"][:, col_b:col_b + 1]
            gc_c[key] = dd["gc"][:, col_g:col_g + 1]
            gc_r = dd["gc_t"][col_g:col_g + 1, :]
            decay = jnp.where(dd["incl"],
                              jnp.exp(jnp.where(dd["incl"], gc_c[key] - gc_r, 0.0)), 0.0)
            m = jnp.where(dd["strict"], kk[d, hv // rep] * beta_c[key] * decay, 0.0)
            x[key] = eye - jnp.where(base_blk, m, 0.0)
            mb[key] = m.astype(BF16)
            attn[key] = jnp.where(dd["incl"], qk[d, hv // rep] * decay, 0.0).astype(BF16)
        for mask in lvl_masks:
            xb = {key: x[key].astype(BF16) for key in grp}
            t = {key: _dot(xb[key], jnp.where(mask, mb[key], 0.0).astype(BF16)).astype(BF16)
                 for key in grp}
            x = {key: x[key] - _dot(t[key], xb[key]) for key in grp}
        sbf, resid = {}, {}
        for key in grp:
            d, hv = key
            dd = dirs[d]
            khf = dd["k"][0, :, head_cols(hv // rep)].astype(F32)
            kg = (khf * beta_c[key] * jnp.exp(gc_c[key])).astype(BF16)
            vb = dd["v"][0, :, head_cols(hv)].astype(F32) * beta_c[key]
            sbf[key] = dd["s"][hv].astype(BF16)
            resid[key] = (vb - _dot(kg, sbf[key])).astype(BF16)
        v_new = {key: _dot(x[key].astype(BF16), resid[key]).astype(BF16) for key in grp}
        for key in grp:
            d, hv = key
            dd = dirs[d]
            qg = (dd["q"][0, :, head_cols(hv // rep)].astype(F32)
                  * jnp.exp(gc_c[key])).astype(BF16)
            dd["o"][0, :, head_cols(hv)] = _dot(
                jnp.concatenate([qg, attn[key]], axis=1),
                jnp.concatenate([sbf[key], v_new[key]], axis=0)).astype(BF16)
        for key in grp:
            d, hv = key
            dd = dirs[d]
            g_last = gc_c[key][dd["last"]:dd["last"] + 1, :]
            khf = dd["k"][0, :, head_cols(hv // rep)].astype(F32)
            kd = (khf * jnp.exp(g_last - gc_c[key])).astype(BF16)
            dd["s"][hv] = dd["s"][hv] * jnp.exp(g_last) + lax.dot_general(
                kd, v_new[key], tn_dims, preferred_element_type=F32)

    @pl.when(i == nc - 1)
    def _():
        sfinf_ref[0] = sf_s[...]
        sfinb_ref[0] = sb_s[...]


def _dn_core_call(q, k, v, gb, s0f, s0b):
    bsz, t_len, _ = q.shape
    nc = t_len // CHUNK
    fwd = lambda b, i: (b, i, 0)
    bwd = lambda b, i: (b, nc - 1 - i, 0)
    state_spec = pl.BlockSpec((1, DN_V_HEADS, DN_HEAD, DN_HEAD), lambda b, i: (b, 0, 0, 0))
    state_shape = jax.ShapeDtypeStruct((bsz, DN_V_HEADS, DN_HEAD, DN_HEAD), F32)

    def in_specs(imap):
        return [pl.BlockSpec((1, CHUNK, DN_KEY_DIM), imap), pl.BlockSpec((1, CHUNK, DN_KEY_DIM), imap),
                pl.BlockSpec((1, CHUNK, DN_VAL_DIM), imap), pl.BlockSpec((1, CHUNK, LANES), imap)]

    return pl.pallas_call(
        functools.partial(_dn_core_kernel, nc=nc),
        grid=(bsz, nc),
        in_specs=in_specs(fwd) + in_specs(bwd) + [state_spec, state_spec],
        out_specs=[pl.BlockSpec((1, CHUNK, DN_VAL_DIM), fwd), pl.BlockSpec((1, CHUNK, DN_VAL_DIM), bwd),
                   state_spec, state_spec],
        out_shape=[jax.ShapeDtypeStruct((bsz, t_len, DN_VAL_DIM), BF16),
                   jax.ShapeDtypeStruct((bsz, t_len, DN_VAL_DIM), BF16), state_shape, state_shape],
        scratch_shapes=[pltpu.VMEM((DN_V_HEADS, DN_HEAD, DN_HEAD), F32),
                        pltpu.VMEM((DN_V_HEADS, DN_HEAD, DN_HEAD), F32)],
        compiler_params=_cparams(2),
        name="dn_core",
    )(q, k, v, gb, q, k, v, gb, s0f, s0b)


def _dn_post_kernel(of_ref, ob_ref, z_ref, x_ref, mod_ref, nw_ref, wo_ref, g2_ref, wgu_ref, wd_ref,
                    *rest, final):
    if final:
        fg_ref, o_ref = rest
    else:
        fg_ref = None
        (o_ref,) = rest
    parts = []
    for hh in range(DN_V_HEADS):
        cols = slice(hh * DN_HEAD, (hh + 1) * DN_HEAD)
        oh = of_ref[0, :, cols].astype(F32) + ob_ref[0, :, cols].astype(F32)
        oh = oh * lax.rsqrt(jnp.mean(oh * oh, axis=-1, keepdims=True) + NORM_EPS) * nw_ref[...]
        parts.append((oh * _silu(z_ref[0, :, cols].astype(F32))).astype(BF16))
    y = _dot(jnp.concatenate(parts, axis=1), wo_ref[...])
    x1 = x_ref[0] + mod_ref[0, 2:3, :] * y
    o_ref[0] = _ffn_residual(x1, mod_ref, g2_ref, wgu_ref, wd_ref, fg_ref)


def _dn_post_call(of, ob, z, x, mod, nw, wo, g2, wgu, wd, final_g, *, tm, is_ctx):
    bsz, t_len, _ = x.shape
    final = final_g is not None
    in_specs = [_tok_spec(tm, DN_VAL_DIM), _tok_spec(tm, DN_VAL_DIM), _tok_spec(tm, DN_VAL_DIM),
                _tok_spec(tm, D_MODEL), _mod_spec(is_ctx), _whole(nw.shape), _whole(wo.shape),
                _whole((1, D_MODEL)), _whole(wgu.shape), _whole(wd.shape)]
    args = [of, ob, z, x, mod, nw, wo, g2, wgu, wd]
    if final:
        in_specs.append(_whole((1, D_MODEL)))
        args.append(final_g)
    return pl.pallas_call(
        functools.partial(_dn_post_kernel, final=final),
        grid=(bsz, t_len // tm),
        in_specs=in_specs,
        out_specs=_tok_spec(tm, D_MODEL),
        out_shape=jax.ShapeDtypeStruct(x.shape, F32),
        compiler_params=_cparams(2),
        name="dn_post",
    )(*args)


def _to_column_major(h, n_rows):
    b, _, d = h.shape
    return h.reshape(b, n_rows, GRID_W, d).transpose(0, 2, 1, 3).reshape(b, -1, d)


def _to_row_major(h, n_rows):
    b, _, d = h.shape
    return h.reshape(b, GRID_W, n_rows, d).transpose(0, 2, 1, 3).reshape(b, -1, d)


def _pad_lanes(a, width=LANES):
    return jnp.pad(a, [(0, 0)] * (a.ndim - 1) + [(0, width - a.shape[-1])])


def kernel(x, c, ctx, c_ctx, ada_w, ada_b, norm_g, final_norm_g, rg_w_in, rg_b_in, rg_conv_w,
           rg_conv_b, rg_gate_w, rg_gate_b, rg_lambda, rg_w_out, rg_b_out, dn_w_in, dn_conv_w,
           dn_a_log, dn_dt_bias, dn_norm_w, dn_w_out, ffn_w_gu, ffn_w_down):
    bsz, t_len, _ = x.shape
    n_rows = t_len // GRID_W
    act = jnp.zeros((MOD_ROWS, D_MODEL), F32).at[0:bsz].set(c).at[CTX_MOD_ROW].set(c_ctx)
    mods = _ada_call(act, ada_w, ada_b).reshape(DEPTH, MOD_ROWS, N_MOD, D_MODEL)
    mods = jnp.pad(mods, ((0, 0), (0, 0), (0, MOD_ROWS - N_MOD), (0, 0)))
    zero_state = jnp.zeros((bsz, SUBLANES, D_MODEL), F32)
    zero_s = jnp.zeros((bsz, DN_V_HEADS, DN_HEAD, DN_HEAD), F32)

    for layer in range(DEPTH):
        last = layer == DEPTH - 1
        mod = mods[layer]
        g1 = norm_g[layer, 0].reshape(1, D_MODEL)
        g2 = norm_g[layer, 1].reshape(1, D_MODEL)
        wgu = ffn_w_gu[layer].astype(BF16)
        wd = ffn_w_down[layer].astype(BF16)
        j = layer // 2
        if layer % 2 == 0:
            wu = rg_w_in[j][:, :D_MODEL].astype(BF16)
            wy = rg_w_in[j][:, D_MODEL:].astype(BF16)
            b_in = rg_b_in[j].reshape(1, 2 * D_MODEL)
            cw = rg_conv_w[j]
            cb = rg_conv_b[j].reshape(1, D_MODEL)
            gw = rg_gate_w[j].astype(BF16)
            gbias = rg_gate_b[j]
            lam = rg_lambda[j]
            wo = rg_w_out[j].astype(BF16)
            bo = rg_b_out[j].reshape(1, D_MODEL)

            def layer_fn(xs, h0f, h0b, tm_a, tm_b, is_ctx):
                hf, ab, bxb, y, hf_fin = _rg_a_call(xs, mod, g1, wu, wy, b_in, cw, cb, gw, gbias,
                                                    lam, h0f, tm=tm_a, is_ctx=is_ctx)
                out, hb_fin = _rg_post_call(xs, hf, ab, bxb, y, h0b, mod, wo, bo, g2, wgu, wd,
                                            tm=tm_b, is_ctx=is_ctx)
                return out, hf_fin, hb_fin

            ctx_new, hf_c, hb_c = layer_fn(ctx, zero_state, zero_state, TM_CTX, TM_CTX, True)
            x, _, _ = layer_fn(x, hf_c, hb_c, TM_PREP, TM_POST, False)
            ctx = ctx_new
        else:
            w_in = dn_w_in[j]
            wqkv = w_in[:, :DN_CONV_DIM].astype(BF16)
            wz = w_in[:, DN_CONV_DIM:DN_CONV_DIM + DN_VAL_DIM].astype(BF16)
            wba = _pad_lanes(w_in[:, DN_CONV_DIM + DN_VAL_DIM:]).astype(BF16)
            cw = dn_conv_w[j]
            n_bd = 2 * DN_V_HEADS
            alog = jnp.zeros((1, LANES), F32).at[0, n_bd:2 * n_bd].set(dn_a_log[j].reshape(-1))
            dtb = jnp.zeros((1, LANES), F32).at[0, n_bd:2 * n_bd].set(dn_dt_bias[j].reshape(-1))
            nw = dn_norm_w[j].reshape(1, DN_HEAD)
            wo = dn_w_out[j].astype(BF16)
            final_g = final_norm_g.reshape(1, D_MODEL) if last else None

            def prep(xs, tm, is_ctx):
                return _dn_prep_call(xs, mod, g1, wqkv, wz, wba, cw, alog, dtb, tm=tm, is_ctx=is_ctx)

            qc, kc, vc, zc, gbc = prep(ctx, TM_CTX, True)
            ofc, obc, sf, sb = _dn_core_call(qc, kc, vc, gbc, zero_s, zero_s)
            x_cm = _to_column_major(x, n_rows)
            q, k, v, z, gb = prep(x_cm, TM_PREP, False)
            of, ob, _, _ = _dn_core_call(q, k, v, gb, sf, sb)
            x_cm = _dn_post_call(of, ob, z, x_cm, mod, nw, wo, g2, wgu, wd, final_g,
                                 tm=TM_POST, is_ctx=False)
            x = _to_row_major(x_cm, n_rows)
            if not last:
                ctx = _dn_post_call(ofc, obc, zc, ctx, mod, nw, wo, g2, wgu, wd, None,
                                    tm=TM_CTX, is_ctx=True)
    return x
```

```python
import functools

import jax
import jax.numpy as jnp
from jax import lax
from jax.experimental import pallas as pl
from jax.experimental.pallas import tpu as pltpu

F32 = jnp.float32
BF16 = jnp.bfloat16

D_MODEL = 1024
DEPTH = 4
GRID_W = 64
N_MOD = 6
NORM_EPS = 1e-6
CONV_W = 4
LRU_HEADS = 4
LRU_BLOCK = D_MODEL // LRU_HEADS
LRU_C = 8.0
DN_QK_HEADS = 8
DN_V_HEADS = 16
DN_HEAD = 128
DN_KEY_DIM = DN_QK_HEADS * DN_HEAD
DN_VAL_DIM = DN_V_HEADS * DN_HEAD
DN_CONV_DIM = 2 * DN_KEY_DIM + DN_VAL_DIM
D_FF = 2816

SUBLANES = 8
LANES = 128
HALO = SUBLANES
CHUNK = 128
DN_INST_GROUP = 16
MOD_ROWS = 8
CTX_MOD_ROW = 4
VMEM_LIMIT_BYTES = 56 * 1024 * 1024
FF_CHUNKS = ((0, 1536), (1536, 1280))
TM_PREP = 512
TM_POST = 256
TM_CTX = 256


def _cparams(n_axes):
    return pltpu.CompilerParams(dimension_semantics=("arbitrary",) * n_axes,
                                vmem_limit_bytes=VMEM_LIMIT_BYTES)


def _whole(shape):
    nd = len(shape)
    return pl.BlockSpec(shape, lambda *_: (0,) * nd, pipeline_mode=pl.Buffered(1))


def _dot(a, b):
    return jnp.dot(a, b, preferred_element_type=F32)


def _sigmoid(v):
    return 0.5 + 0.5 * jnp.tanh(0.5 * v)


def _silu(v):
    hv = 0.5 * v
    return hv + hv * jnp.tanh(hv)


F32_TINY = float(jnp.finfo(jnp.float32).tiny)
GELU_C1 = 0.7978845608028654
GELU_C2 = 0.044715 * GELU_C1


def _gelu_tanh(v):
    hv = 0.5 * v
    return hv + hv * jnp.tanh(v * (GELU_C1 + GELU_C2 * (v * v)))


def _softplus(v):
    return jnp.maximum(v, 0.0) + jnp.log1p(jnp.exp(-jnp.abs(v)))


def _log_sigmoid(v):
    return jnp.minimum(v, 0.0) - jnp.log1p(jnp.exp(-jnp.abs(v)))


def _rmsnorm(x, g):
    return x * lax.rsqrt(jnp.mean(x * x, axis=-1, keepdims=True) + NORM_EPS) * g


def _norm_mod(x, g, shift, scale):
    return _rmsnorm(x, g) * (1.0 + scale) + shift


def _mod_spec(is_ctx):
    if is_ctx:
        return pl.BlockSpec((1, MOD_ROWS, D_MODEL), lambda b, i: (CTX_MOD_ROW, 0, 0))
    return pl.BlockSpec((1, MOD_ROWS, D_MODEL), lambda b, i: (b, 0, 0))


def _tok_spec(tm, width):
    return pl.BlockSpec((1, tm, width), lambda b, i: (b, i, 0))


def _ada_kernel(act_ref, w_ref, b_ref, o_ref):
    a = _silu(act_ref[...])
    o_ref[0] = _dot(a, w_ref[0]) + b_ref[0]


def _ada_call(act, ada_w, ada_b):
    return pl.pallas_call(
        _ada_kernel,
        grid=(DEPTH, N_MOD),
        in_specs=[pl.BlockSpec((MOD_ROWS, D_MODEL), lambda l, j: (0, 0)),
                  pl.BlockSpec((1, D_MODEL, D_MODEL), lambda l, j: (l, 0, j)),
                  pl.BlockSpec((1, 1, D_MODEL), lambda l, j: (l, 0, j))],
        out_specs=pl.BlockSpec((1, MOD_ROWS, D_MODEL), lambda l, j: (l, 0, j)),
        out_shape=jax.ShapeDtypeStruct((DEPTH, MOD_ROWS, N_MOD * D_MODEL), F32),
        compiler_params=_cparams(2),
        name="ada",
    )(act, ada_w, ada_b.reshape(DEPTH, 1, N_MOD * D_MODEL))


def _ffn_residual(x, mod_ref, g_ref, wgu_ref, wd_ref, fg_ref):
    h = _norm_mod(x, g_ref[...], mod_ref[0, 3:4, :], mod_ref[0, 4:5, :]).astype(BF16)
    acc = None
    for c0, cw in FF_CHUNKS:
        gate = _dot(h, wgu_ref[:, c0:c0 + cw])
        up = _dot(h, wgu_ref[:, D_FF + c0:D_FF + c0 + cw])
        part = _dot((_silu(gate) * up).astype(BF16), wd_ref[c0:c0 + cw, :])
        acc = part if acc is None else acc + part
    y = x + mod_ref[0, 5:6, :] * acc
    if fg_ref is not None:
        y = _rmsnorm(y, fg_ref[...])
    return y


def _halo_specs(tm, t_len, halo_lo=HALO):
    per_lo = tm // halo_lo
    per_hi = tm // HALO
    last = t_len // HALO - 1
    prev = pl.BlockSpec((1, halo_lo, D_MODEL), lambda b, i: (b, jnp.maximum(i * per_lo - 1, 0), 0))
    nxt = pl.BlockSpec((1, HALO, D_MODEL), lambda b, i: (b, jnp.minimum((i + 1) * per_hi, last), 0))
    return prev, nxt


def _load_normed_with_halo(xm_ref, xp_ref, xn_ref, xcat, g, shift, scale, tm, halo_lo=HALO):
    xcat[0:halo_lo, :] = xp_ref[0]
    xcat[halo_lo:halo_lo + tm, :] = xm_ref[0]
    xcat[halo_lo + tm:halo_lo + tm + HALO, :] = xn_ref[0]
    return _norm_mod(xcat[...], g, shift, scale).astype(BF16)


def _group_iota():
    return lax.broadcasted_iota(jnp.int32, (1, SUBLANES, 1), 1)


def _dwconv(p, cw, i, nt, tm):
    n_grp = tm // SUBLANES
    c = p.shape[-1]
    p3 = p.reshape(n_grp + 2, SUBLANES, c)
    lo = p3[0:1] * jnp.where(i > 0, 1.0, 0.0)
    hi = p3[n_grp + 1:n_grp + 2] * jnp.where(i < nt - 1, 1.0, 0.0)
    p3 = jnp.concatenate([lo, p3[1:n_grp + 1], hi], axis=0)
    r = _group_iota()
    r2 = pltpu.roll(p3, 2, 1)
    r1 = pltpu.roll(p3, 1, 1)
    r7 = pltpu.roll(p3, SUBLANES - 1, 1)
    tap_m2 = jnp.where(r >= 2, r2[1:n_grp + 1], r2[0:n_grp])
    tap_m1 = jnp.where(r >= 1, r1[1:n_grp + 1], r1[0:n_grp])
    tap_p1 = jnp.where(r < SUBLANES - 1, r7[1:n_grp + 1], r7[2:n_grp + 2])
    w = [cw[j:j + 1, :].reshape(1, 1, c) for j in range(CONV_W)]
    out = w[0] * tap_m2 + w[1] * tap_m1 + w[2] * p3[1:n_grp + 1] + w[3] * tap_p1
    return out.reshape(tm, c)


CHUNK_GROUPS = CHUNK // SUBLANES
DN_HALO_LO = 2 * SUBLANES


def _dwconv_interleaved(p, cw, i, nt, tm):
    c = p.shape[-1]
    n_chunk = tm // CHUNK
    n_grp = n_chunk * CHUNK_GROUPS
    p3 = p.reshape(2 + n_grp + 1, SUBLANES, c)
    lo = p3[0:2] * jnp.where(i > 0, 1.0, 0.0)
    hi = p3[n_grp + 2:n_grp + 3] * jnp.where(i < nt - 1, 1.0, 0.0)
    p3 = jnp.concatenate([lo, p3[2:n_grp + 2], hi], axis=0)
    r = _group_iota()

    def grp(k):
        return p3[k + 2:k + 3]

    def from_prev_sublane(cur, prev):
        return jnp.where(r >= 1, pltpu.roll(grp(cur), 1, 1), pltpu.roll(grp(prev), 1, 1))

    m1, m2, p1 = [], [], []
    for ch in range(n_chunk):
        b = ch * CHUNK_GROUPS
        wrap_m1 = from_prev_sublane(b + CHUNK_GROUPS - 1, b - 1)
        wrap_m2 = from_prev_sublane(b + CHUNK_GROUPS - 2, b - 2)
        wrap_p1 = jnp.where(r < SUBLANES - 1, pltpu.roll(grp(b), SUBLANES - 1, 1),
                            pltpu.roll(grp(b + CHUNK_GROUPS), SUBLANES - 1, 1))
        m1 += [wrap_m1, p3[b + 2:b + 2 + CHUNK_GROUPS - 1]]
        m2 += [wrap_m2, wrap_m1, p3[b + 2:b + 2 + CHUNK_GROUPS - 2]]
        p1 += [p3[b + 3:b + 2 + CHUNK_GROUPS], wrap_p1]
    w = [cw[j:j + 1, :].reshape(1, 1, c) for j in range(CONV_W)]
    out = (w[0] * jnp.concatenate(m2, axis=0) + w[1] * jnp.concatenate(m1, axis=0)
           + w[2] * p3[2:n_grp + 2] + w[3] * jnp.concatenate(p1, axis=0))
    return out.reshape(tm, c)


def _local_scan(a, b, tm, reverse):
    a3 = a.reshape(tm // SUBLANES, SUBLANES, D_MODEL)
    b3 = b.reshape(tm // SUBLANES, SUBLANES, D_MODEL)
    r = _group_iota()
    for s in (1, 2, 4):
        shift = SUBLANES - s if reverse else s
        a_sh = pltpu.roll(a3, shift, 1)
        b_sh = pltpu.roll(b3, shift, 1)
        valid = (r < SUBLANES - s) if reverse else (r >= s)
        b3 = jnp.where(valid, a3 * b_sh + b3, b3)
        a3 = jnp.where(valid, a3 * a_sh, a3)
    return a3.reshape(tm, D_MODEL), b3.reshape(tm, D_MODEL)


def _carry_scan(a_s, b_s, write, h, tm, reverse):
    n_grp = tm // SUBLANES

    def body(t, h):
        g = (n_grp - 1 - t) if reverse else t
        r0 = pl.multiple_of(g * SUBLANES, SUBLANES)
        hh = a_s[pl.ds(r0, SUBLANES), :] * h + b_s[pl.ds(r0, SUBLANES), :]
        write(r0, hh)
        return hh[0:1, :] if reverse else hh[SUBLANES - 1:SUBLANES, :]

    return lax.fori_loop(0, n_grp, body, h, unroll=8)


def _rg_a_kernel(xm_ref, xp_ref, xn_ref, mod_ref, g_ref, wu_ref, wy_ref, bin_ref, cw_ref, cb_ref,
                 gw_ref, gb_ref, lam_ref, h0_ref,
                 hf_ref, ab_ref, bxb_ref, y_ref, hfin_ref,
                 xcat, a_s, b_s, carry, *, tm, nt):
    i = pl.program_id(1)
    h = _load_normed_with_halo(xm_ref, xp_ref, xn_ref, xcat, g_ref[...],
                               mod_ref[0, 0:1, :], mod_ref[0, 1:2, :], tm)
    pu = _dot(h, wu_ref[...]) + bin_ref[:, 0:D_MODEL]
    u = _dwconv(pu, cw_ref[...], i, nt, tm) + cb_ref[...]
    py = _dot(h[HALO:HALO + tm, :], wy_ref[...]) + bin_ref[:, D_MODEL:2 * D_MODEL]
    y_ref[0] = _gelu_tanh(py).astype(BF16)

    ub = u.astype(BF16)
    half_u = 0.5 * u
    half_c = (0.5 * LRU_C) * _log_sigmoid(lam_ref[...])
    for d in (0, 1):
        pre = []
        for gi in (0, 1):
            parts = [_dot(ub[:, hh * LRU_BLOCK:(hh + 1) * LRU_BLOCK], gw_ref[d, gi, hh])
                     for hh in range(LRU_HEADS)]
            pre.append(jnp.concatenate(parts, axis=1) + gb_ref[d, gi:gi + 1, :])
        t_r = jnp.tanh(pre[0])
        t_i = jnp.tanh(pre[1])
        hc = half_c[d:d + 1, :]
        log_a = hc + hc * t_r
        a = jnp.exp(log_a)
        t = jnp.tanh(log_a)
        q = (-2.0 * t) / (1.0 - t)
        root = q * lax.rsqrt(jnp.maximum(q, F32_TINY))
        bx = root * (half_u + half_u * t_i)
        if d == 1:
            ab_ref[0] = a
            bxb_ref[0] = bx
        else:
            a_loc, b_loc = _local_scan(a, bx, tm, reverse=False)
            a_s[...] = a_loc
            b_s[...] = b_loc

    @pl.when(i == 0)
    def _():
        carry[...] = h0_ref[0]

    def write(r0, hh):
        hf_ref[0, pl.ds(r0, SUBLANES), :] = hh

    h_end = _carry_scan(a_s, b_s, write, carry[0:1, :], tm, reverse=False)
    carry[...] = jnp.broadcast_to(h_end, (SUBLANES, D_MODEL))

    @pl.when(i == nt - 1)
    def _():
        hfin_ref[0] = carry[...]


def _rg_a_call(x, mod, g, wu, wy, b_in, cw, cb, gw, gb, lam, h0, *, tm, is_ctx):
    bsz, t_len, _ = x.shape
    nt = t_len // tm
    prev, nxt = _halo_specs(tm, t_len)
    state_spec = pl.BlockSpec((1, SUBLANES, D_MODEL), lambda b, i: (b, 0, 0))
    tok = _tok_spec(tm, D_MODEL)
    return pl.pallas_call(
        functools.partial(_rg_a_kernel, tm=tm, nt=nt),
        grid=(bsz, nt),
        in_specs=[tok, prev, nxt, _mod_spec(is_ctx), _whole((1, D_MODEL)),
                  _whole(wu.shape), _whole(wy.shape), _whole(b_in.shape), _whole(cw.shape),
                  _whole(cb.shape), _whole(gw.shape), _whole(gb.shape), _whole(lam.shape),
                  state_spec],
        out_specs=[tok, tok, tok, tok, state_spec],
        out_shape=[jax.ShapeDtypeStruct(x.shape, F32), jax.ShapeDtypeStruct(x.shape, F32),
                   jax.ShapeDtypeStruct(x.shape, F32), jax.ShapeDtypeStruct(x.shape, BF16),
                   jax.ShapeDtypeStruct((bsz, SUBLANES, D_MODEL), F32)],
        scratch_shapes=[pltpu.VMEM((tm + 2 * HALO, D_MODEL), F32),
                        pltpu.VMEM((tm, D_MODEL), F32), pltpu.VMEM((tm, D_MODEL), F32),
                        pltpu.VMEM((SUBLANES, D_MODEL), F32)],
        compiler_params=_cparams(2),
        name="rg_a",
    )(x, x, x, mod, g, wu, wy, b_in, cw, cb, gw, gb, lam, h0)


def _rg_post_kernel(x_ref, hf_ref, ab_ref, bxb_ref, y_ref, h0_ref, mod_ref, wo_ref, bo_ref,
                    g2_ref, wgu_ref, wd_ref, o_ref, hfin_ref, a_s, b_s, carry, *, tm, nt):
    i = pl.program_id(1)
    a_loc, b_loc = _local_scan(ab_ref[0], bxb_ref[0], tm, reverse=True)
    a_s[...] = a_loc
    b_s[...] = b_loc

    @pl.when(i == 0)
    def _():
        carry[...] = h0_ref[0]

    def write(r0, hh):
        b_s[pl.ds(r0, SUBLANES), :] = hh

    h_end = _carry_scan(a_s, b_s, write, carry[0:1, :], tm, reverse=True)
    carry[...] = jnp.broadcast_to(h_end, (SUBLANES, D_MODEL))

    @pl.when(i == nt - 1)
    def _():
        hfin_ref[0] = carry[...]

    hl = hf_ref[0] + b_s[...]
    m = (hl * y_ref[0].astype(F32)).astype(BF16)
    out = _dot(m, wo_ref[...]) + bo_ref[...]
    x1 = x_ref[0] + mod_ref[0, 2:3, :] * out
    o_ref[0] = _ffn_residual(x1, mod_ref, g2_ref, wgu_ref, wd_ref, None)


def _rg_post_call(x, hf, ab, bxb, y, h0, mod, wo, bo, g2, wgu, wd, *, tm, is_ctx):
    bsz, t_len, _ = x.shape
    nt = t_len // tm
    tok = pl.BlockSpec((1, tm, D_MODEL), lambda b, i: (b, nt - 1 - i, 0))
    state_spec = pl.BlockSpec((1, SUBLANES, D_MODEL), lambda b, i: (b, 0, 0))
    return pl.pallas_call(
        functools.partial(_rg_post_kernel, tm=tm, nt=nt),
        grid=(bsz, nt),
        in_specs=[tok, tok, tok, tok, tok, state_spec, _mod_spec(is_ctx),
                  _whole(wo.shape), _whole(bo.shape), _whole((1, D_MODEL)),
                  _whole(wgu.shape), _whole(wd.shape)],
        out_specs=[tok, state_spec],
        out_shape=[jax.ShapeDtypeStruct(x.shape, F32),
                   jax.ShapeDtypeStruct((bsz, SUBLANES, D_MODEL), F32)],
        scratch_shapes=[pltpu.VMEM((tm, D_MODEL), F32), pltpu.VMEM((tm, D_MODEL), F32),
                        pltpu.VMEM((SUBLANES, D_MODEL), F32)],
        compiler_params=_cparams(2),
        name="rg_post",
    )(x, hf, ab, bxb, y, h0, mod, wo, bo, g2, wgu, wd)


def _dn_prep_kernel(xm_ref, xp_ref, xn_ref, mod_ref, g_ref, wqkv_ref, wz_ref, wba_ref, cw_ref,
                    alog_ref, dtb_ref,
                    q_ref, k_ref, v_ref, z_ref, gb_ref,
                    xcat, *, tm, nt):
    i = pl.program_id(1)
    h = _load_normed_with_halo(xm_ref, xp_ref, xn_ref, xcat, g_ref[...],
                               mod_ref[0, 0:1, :], mod_ref[0, 1:2, :], tm, DN_HALO_LO)
    n_slab = DN_CONV_DIM // D_MODEL
    for c in range(n_slab):
        cols = slice(c * D_MODEL, (c + 1) * D_MODEL)
        hv = _dwconv_interleaved(_dot(h, wqkv_ref[:, cols]), cw_ref[:, cols], i, nt, tm)
        u = hv + hv * jnp.tanh(hv)
        if c < 2:
            dst = q_ref if c == 0 else k_ref
            scale = DN_HEAD ** -0.5 if c == 0 else 1.0
            for hh in range(DN_QK_HEADS):
                uh = u[:, hh * DN_HEAD:(hh + 1) * DN_HEAD]
                inv = lax.rsqrt(jnp.sum(uh * uh, axis=-1, keepdims=True) + NORM_EPS) * scale
                dst[0, :, hh * DN_HEAD:(hh + 1) * DN_HEAD] = (uh * inv).astype(BF16)
        else:
            v_ref[0, :, (c - 2) * D_MODEL:(c - 1) * D_MODEL] = u.astype(BF16)

    hm = h[DN_HALO_LO:DN_HALO_LO + tm, :]
    z_ref[0] = _dot(hm, wz_ref[...]).astype(BF16)
    ba = _dot(hm, wba_ref[...])
    lane = lax.broadcasted_iota(jnp.int32, (1, LANES), 1)
    beta = _sigmoid(ba)
    gdec = -jnp.exp(alog_ref[...]) * _softplus(ba + dtb_ref[...])
    n_bd = 2 * DN_V_HEADS
    gb_ref[0] = jnp.where(lane < n_bd, beta, jnp.where(lane < 2 * n_bd, gdec, 0.0))


def _dn_prep_call(x, mod, g, wqkv, wz, wba, cw, alog, dtb, *, tm, is_ctx):
    bsz, t_len, _ = x.shape
    nt = t_len // tm
    prev, nxt = _halo_specs(tm, t_len, DN_HALO_LO)
    return pl.pallas_call(
        functools.partial(_dn_prep_kernel, tm=tm, nt=nt),
        grid=(bsz, nt),
        in_specs=[_tok_spec(tm, D_MODEL), prev, nxt, _mod_spec(is_ctx), _whole((1, D_MODEL)),
                  _whole(wqkv.shape), _whole(wz.shape), _whole(wba.shape), _whole(cw.shape),
                  _whole(alog.shape), _whole(dtb.shape)],
        out_specs=[_tok_spec(tm, DN_KEY_DIM), _tok_spec(tm, DN_KEY_DIM), _tok_spec(tm, DN_VAL_DIM),
                   _tok_spec(tm, DN_VAL_DIM), _tok_spec(tm, LANES)],
        out_shape=[jax.ShapeDtypeStruct((bsz, t_len, DN_KEY_DIM), BF16),
                   jax.ShapeDtypeStruct((bsz, t_len, DN_KEY_DIM), BF16),
                   jax.ShapeDtypeStruct((bsz, t_len, DN_VAL_DIM), BF16),
                   jax.ShapeDtypeStruct((bsz, t_len, DN_VAL_DIM), BF16),
                   jax.ShapeDtypeStruct((bsz, t_len, LANES), F32)],
        scratch_shapes=[pltpu.VMEM((tm + DN_HALO_LO + HALO, D_MODEL), F32)],
        compiler_params=_cparams(2),
        name="dn_prep",
    )(x, x, x, mod, g, wqkv, wz, wba, cw, alog, dtb)


def _level_masks(row, col):
    masks = []
    lvl = 1
    while (1 << lvl) < CHUNK:
        same = (row >> lvl) == (col >> lvl)
        same2 = (row >> (lvl + 1)) == (col >> (lvl + 1))
        masks.append(jnp.logical_and(same2, jnp.logical_not(same)))
        lvl += 1
    return masks


def _dn_core_kernel(qf_ref, kf_ref, vf_ref, gbf_ref, qb_ref, kb_ref, vb_ref, gbb_ref,
                    s0f_ref, s0b_ref, of_ref, ob_ref, sfinf_ref, sfinb_ref, sf_s, sb_s, *, nc):
    i = pl.program_id(1)

    @pl.when(i == 0)
    def _():
        sf_s[...] = s0f_ref[0]
        sb_s[...] = s0b_ref[0]

    def token_of(phys):
        return (phys & (SUBLANES - 1)) * CHUNK_GROUPS + (phys >> 3)

    row = token_of(lax.broadcasted_iota(jnp.int32, (CHUNK, CHUNK), 0))
    col = token_of(lax.broadcasted_iota(jnp.int32, (CHUNK, CHUNK), 1))
    base_blk = (row >> 1) == (col >> 1)
    eye = jnp.where(row == col, 1.0, 0.0)
    lvl_masks = _level_masks(row, col)
    nt_dims = (((1,), (1,)), ((), ()))
    tn_dims = (((0,), (0,)), ((), ()))
    rep = DN_V_HEADS // DN_QK_HEADS

    def head_cols(hv):
        return slice(hv * DN_HEAD, (hv + 1) * DN_HEAD)

    dirs = []
    for d, (q_ref, k_ref, v_ref, gb_ref, o_ref, s_ref) in enumerate(
            ((qf_ref, kf_ref, vf_ref, gbf_ref, of_ref, sf_s),
             (qb_ref, kb_ref, vb_ref, gbb_ref, ob_ref, sb_s))):
        incl = (row <= col) if d == 1 else (row >= col)
        strict = (row < col) if d == 1 else (row > col)
        gb = gb_ref[0]
        gc_all = jnp.dot(incl.astype(F32), gb, preferred_element_type=F32,
                         precision=lax.Precision.HIGHEST)
        dirs.append(dict(q=q_ref, k=k_ref, v=v_ref, o=o_ref, s=s_ref, incl=incl, strict=strict,
                         gb=gb, gc=gc_all, gc_t=gc_all.T, last=0 if d == 1 else CHUNK - 1))

    insts = [(d, hv) for d in (0, 1) for hv in range(DN_V_HEADS)]
    for g0 in range(0, len(insts), DN_INST_GROUP):
        grp = insts[g0:g0 + DN_INST_GROUP]
        kk, qk = {}, {}
        for d, hq in sorted({(d, hv // rep) for d, hv in grp}):
            kh = dirs[d]["k"][0, :, head_cols(hq)]
            qh = dirs[d]["q"][0, :, head_cols(hq)]
            both = lax.dot_general(jnp.concatenate([kh, qh], axis=0), kh, nt_dims,
                                   preferred_element_type=F32)
            kk[d, hq] = both[0:CHUNK]
            qk[d, hq] = both[CHUNK:2 * CHUNK]
        beta_c, gc_c, mb, attn, x = {}, {}, {}, {}, {}
        for key in grp:
            d, hv = key
            dd = dirs[d]
            col_b = DN_V_HEADS * d + hv
            col_g = 2 * DN_V_HEADS + DN_V_HEADS * d + hv
            beta_c[key] = dd["gb"][:, col_b:col_b + 1]
            gc_c[key] = dd["gc"][:, col_g:col_g + 1]
            gc_r = dd["gc_t"][col_g:col_g + 1, :]
            decay = jnp.where(dd["incl"],
                              jnp.exp(jnp.where(dd["incl"], gc_c[key] - gc_r, 0.0)), 0.0)
            m = jnp.where(dd["strict"], kk[d, hv // rep] * beta_c[key] * decay, 0.0)
            x[key] = eye - jnp.where(base_blk, m, 0.0)
            mb[key] = m.astype(BF16)
            attn[key] = jnp.where(dd["incl"], qk[d, hv // rep] * decay, 0.0).astype(BF16)
        for mask in lvl_masks:
            xb = {key: x[key].astype(BF16) for key in grp}
            t = {key: _dot(xb[key], jnp.where(mask, mb[key], 0.0).astype(BF16)).astype(BF16)
                 for key in grp}
            x = {key: x[key] - _dot(t[key], xb[key]) for key in grp}
        sbf, resid = {}, {}
        for key in grp:
            d, hv = key
            dd = dirs[d]
            khf = dd["k"][0, :, head_cols(hv // rep)].astype(F32)
            kg = (khf * beta_c[key] * jnp.exp(gc_c[key])).astype(BF16)
            vb = dd["v"][0, :, head_cols(hv)].astype(F32) * beta_c[key]
            sbf[key] = dd["s"][hv].astype(BF16)
            resid[key] = (vb - _dot(kg, sbf[key])).astype(BF16)
        v_new = {key: _dot(x[key].astype(BF16), resid[key]).astype(BF16) for key in grp}
        for key in grp:
            d, hv = key
            dd = dirs[d]
            qg = (dd["q"][0, :, head_cols(hv // rep)].astype(F32)
                  * jnp.exp(gc_c[key])).astype(BF16)
            dd["o"][0, :, head_cols(hv)] = _dot(
                jnp.concatenate([qg, attn[key]], axis=1),
                jnp.concatenate([sbf[key], v_new[key]], axis=0)).astype(BF16)
        for key in grp:
            d, hv = key
            dd = dirs[d]
            g_last = gc_c[key][dd["last"]:dd["last"] + 1, :]
            khf = dd["k"][0, :, head_cols(hv // rep)].astype(F32)
            kd = (khf * jnp.exp(g_last - gc_c[key])).astype(BF16)
            dd["s"][hv] = dd["s"][hv] * jnp.exp(g_last) + lax.dot_general(
                kd, v_new[key], tn_dims, preferred_element_type=F32)

    @pl.when(i == nc - 1)
    def _():
        sfinf_ref[0] = sf_s[...]
        sfinb_ref[0] = sb_s[...]


def _dn_core_call(q, k, v, gb, s0f, s0b):
    bsz, t_len, _ = q.shape
    nc = t_len // CHUNK
    fwd = lambda b, i: (b, i, 0)
    bwd = lambda b, i: (b, nc - 1 - i, 0)
    state_spec = pl.BlockSpec((1, DN_V_HEADS, DN_HEAD, DN_HEAD), lambda b, i: (b, 0, 0, 0))
    state_shape = jax.ShapeDtypeStruct((bsz, DN_V_HEADS, DN_HEAD, DN_HEAD), F32)

    def in_specs(imap):
        return [pl.BlockSpec((1, CHUNK, DN_KEY_DIM), imap), pl.BlockSpec((1, CHUNK, DN_KEY_DIM), imap),
                pl.BlockSpec((1, CHUNK, DN_VAL_DIM), imap), pl.BlockSpec((1, CHUNK, LANES), imap)]

    return pl.pallas_call(
        functools.partial(_dn_core_kernel, nc=nc),
        grid=(bsz, nc),
        in_specs=in_specs(fwd) + in_specs(bwd) + [state_spec, state_spec],
        out_specs=[pl.BlockSpec((1, CHUNK, DN_VAL_DIM), fwd), pl.BlockSpec((1, CHUNK, DN_VAL_DIM), bwd),
                   state_spec, state_spec],
        out_shape=[jax.ShapeDtypeStruct((bsz, t_len, DN_VAL_DIM), BF16),
                   jax.ShapeDtypeStruct((bsz, t_len, DN_VAL_DIM), BF16), state_shape, state_shape],
        scratch_shapes=[pltpu.VMEM((DN_V_HEADS, DN_HEAD, DN_HEAD), F32),
                        pltpu.VMEM((DN_V_HEADS, DN_HEAD, DN_HEAD), F32)],
        compiler_params=_cparams(2),
        name="dn_core",
    )(q, k, v, gb, q, k, v, gb, s0f, s0b)


def _dn_post_kernel(of_ref, ob_ref, z_ref, x_ref, mod_ref, nw_ref, wo_ref, g2_ref, wgu_ref, wd_ref,
                    *rest, final):
    if final:
        fg_ref, o_ref = rest
    else:
        fg_ref = None
        (o_ref,) = rest
    parts = []
    for hh in range(DN_V_HEADS):
        cols = slice(hh * DN_HEAD, (hh + 1) * DN_HEAD)
        oh = of_ref[0, :, cols].astype(F32) + ob_ref[0, :, cols].astype(F32)
        oh = oh * lax.rsqrt(jnp.mean(oh * oh, axis=-1, keepdims=True) + NORM_EPS) * nw_ref[...]
        parts.append((oh * _silu(z_ref[0, :, cols].astype(F32))).astype(BF16))
    y = _dot(jnp.concatenate(parts, axis=1), wo_ref[...])
    x1 = x_ref[0] + mod_ref[0, 2:3, :] * y
    o_ref[0] = _ffn_residual(x1, mod_ref, g2_ref, wgu_ref, wd_ref, fg_ref)


def _dn_post_call(of, ob, z, x, mod, nw, wo, g2, wgu, wd, final_g, *, tm, is_ctx):
    bsz, t_len, _ = x.shape
    final = final_g is not None
    in_specs = [_tok_spec(tm, DN_VAL_DIM), _tok_spec(tm, DN_VAL_DIM), _tok_spec(tm, DN_VAL_DIM),
                _tok_spec(tm, D_MODEL), _mod_spec(is_ctx), _whole(nw.shape), _whole(wo.shape),
                _whole((1, D_MODEL)), _whole(wgu.shape), _whole(wd.shape)]
    args = [of, ob, z, x, mod, nw, wo, g2, wgu, wd]
    if final:
        in_specs.append(_whole((1, D_MODEL)))
        args.append(final_g)
    return pl.pallas_call(
        functools.partial(_dn_post_kernel, final=final),
        grid=(bsz, t_len // tm),
        in_specs=in_specs,
        out_specs=_tok_spec(tm, D_MODEL),
        out_shape=jax.ShapeDtypeStruct(x.shape, F32),
        compiler_params=_cparams(2),
        name="dn_post",
    )(*args)


def _to_column_major(h):
    b, t_len, d = h.shape
    assert t_len == CHUNK * GRID_W
    h = h.reshape(b, SUBLANES, CHUNK_GROUPS, GRID_W, d)
    return h.transpose(0, 3, 2, 1, 4).reshape(b, t_len, d)


def _to_row_major(h):
    b, t_len, d = h.shape
    h = h.reshape(b, GRID_W, CHUNK_GROUPS, SUBLANES, d)
    return h.transpose(0, 3, 2, 1, 4).reshape(b, t_len, d)


def _interleave_chunks(h):
    b, t_len, d = h.shape
    h = h.reshape(b, t_len // CHUNK, SUBLANES, CHUNK_GROUPS, d)
    return h.transpose(0, 1, 3, 2, 4).reshape(b, t_len, d)


def _deinterleave_chunks(h):
    b, t_len, d = h.shape
    h = h.reshape(b, t_len // CHUNK, CHUNK_GROUPS, SUBLANES, d)
    return h.transpose(0, 1, 3, 2, 4).reshape(b, t_len, d)


def _pad_lanes(a, width=LANES):
    return jnp.pad(a, [(0, 0)] * (a.ndim - 1) + [(0, width - a.shape[-1])])


def kernel(x, c, ctx, c_ctx, ada_w, ada_b, norm_g, final_norm_g, rg_w_in, rg_b_in, rg_conv_w,
           rg_conv_b, rg_gate_w, rg_gate_b, rg_lambda, rg_w_out, rg_b_out, dn_w_in, dn_conv_w,
           dn_a_log, dn_dt_bias, dn_norm_w, dn_w_out, ffn_w_gu, ffn_w_down):
    bsz = x.shape[0]
    act = jnp.zeros((MOD_ROWS, D_MODEL), F32).at[0:bsz].set(c).at[CTX_MOD_ROW].set(c_ctx)
    mods = _ada_call(act, ada_w, ada_b).reshape(DEPTH, MOD_ROWS, N_MOD, D_MODEL)
    mods = jnp.pad(mods, ((0, 0), (0, 0), (0, MOD_ROWS - N_MOD), (0, 0)))
    zero_state = jnp.zeros((bsz, SUBLANES, D_MODEL), F32)
    zero_s = jnp.zeros((bsz, DN_V_HEADS, DN_HEAD, DN_HEAD), F32)

    for layer in range(DEPTH):
        last = layer == DEPTH - 1
        mod = mods[layer]
        g1 = norm_g[layer, 0].reshape(1, D_MODEL)
        g2 = norm_g[layer, 1].reshape(1, D_MODEL)
        wgu = ffn_w_gu[layer].astype(BF16)
        wd = ffn_w_down[layer].astype(BF16)
        j = layer // 2
        if layer % 2 == 0:
            wu = rg_w_in[j][:, :D_MODEL].astype(BF16)
            wy = rg_w_in[j][:, D_MODEL:].astype(BF16)
            b_in = rg_b_in[j].reshape(1, 2 * D_MODEL)
            cw = rg_conv_w[j]
            cb = rg_conv_b[j].reshape(1, D_MODEL)
            gw = (0.5 * rg_gate_w[j]).astype(BF16)
            gbias = 0.5 * rg_gate_b[j]
            lam = rg_lambda[j]
            wo = rg_w_out[j].astype(BF16)
            bo = rg_b_out[j].reshape(1, D_MODEL)

            def layer_fn(xs, h0f, h0b, tm_a, tm_b, is_ctx):
                hf, ab, bxb, y, hf_fin = _rg_a_call(xs, mod, g1, wu, wy, b_in, cw, cb, gw, gbias,
                                                    lam, h0f, tm=tm_a, is_ctx=is_ctx)
                out, hb_fin = _rg_post_call(xs, hf, ab, bxb, y, h0b, mod, wo, bo, g2, wgu, wd,
                                            tm=tm_b, is_ctx=is_ctx)
                return out, hf_fin, hb_fin

            ctx_new, hf_c, hb_c = layer_fn(ctx, zero_state, zero_state, TM_CTX, TM_CTX, True)
            x, _, _ = layer_fn(x, hf_c, hb_c, TM_PREP, TM_POST, False)
            ctx = ctx_new
        else:
            w_in = dn_w_in[j]
            wqkv = w_in[:, :DN_CONV_DIM].astype(BF16)
            wz = w_in[:, DN_CONV_DIM:DN_CONV_DIM + DN_VAL_DIM].astype(BF16)
            wba = _pad_lanes(w_in[:, DN_CONV_DIM + DN_VAL_DIM:]).astype(BF16)
            cw = 0.5 * dn_conv_w[j]
            n_bd = 2 * DN_V_HEADS
            alog = jnp.zeros((1, LANES), F32).at[0, n_bd:2 * n_bd].set(dn_a_log[j].reshape(-1))
            dtb = jnp.zeros((1, LANES), F32).at[0, n_bd:2 * n_bd].set(dn_dt_bias[j].reshape(-1))
            nw = dn_norm_w[j].reshape(1, DN_HEAD)
            wo = dn_w_out[j].astype(BF16)
            final_g = final_norm_g.reshape(1, D_MODEL) if last else None

            def prep(xs, tm, is_ctx):
                return _dn_prep_call(xs, mod, g1, wqkv, wz, wba, cw, alog, dtb, tm=tm, is_ctx=is_ctx)

            ctx_il = _interleave_chunks(ctx)
            qc, kc, vc, zc, gbc = prep(ctx_il, TM_CTX, True)
            ofc, obc, sf, sb = _dn_core_call(qc, kc, vc, gbc, zero_s, zero_s)
            x_cm = _to_column_major(x)
            q, k, v, z, gb = prep(x_cm, TM_PREP, False)
            of, ob, _, _ = _dn_core_call(q, k, v, gb, sf, sb)
            x_cm = _dn_post_call(of, ob, z, x_cm, mod, nw, wo, g2, wgu, wd, final_g,
                                 tm=TM_POST, is_ctx=False)
            x = _to_row_major(x_cm)
            if not last:
                ctx = _deinterleave_chunks(
                    _dn_post_call(ofc, obc, zc, ctx_il, mod, nw, wo, g2, wgu, wd, None,
                                  tm=TM_CTX, is_ctx=True))
    return x
```

```python
import functools

import jax
import jax.numpy as jnp
from jax import lax
from jax.experimental import pallas as pl
from jax.experimental.pallas import tpu as pltpu

F32 = jnp.float32
BF16 = jnp.bfloat16

D_MODEL = 1024
DEPTH = 4
GRID_W = 64
N_MOD = 6
NORM_EPS = 1e-6
CONV_W = 4
LRU_HEADS = 4
LRU_BLOCK = D_MODEL // LRU_HEADS
LRU_C = 8.0
DN_QK_HEADS = 8
DN_V_HEADS = 16
DN_HEAD = 128
DN_KEY_DIM = DN_QK_HEADS * DN_HEAD
DN_VAL_DIM = DN_V_HEADS * DN_HEAD
DN_CONV_DIM = 2 * DN_KEY_DIM + DN_VAL_DIM
D_FF = 2816

SUBLANES = 8
LANES = 128
HALO = SUBLANES
CHUNK = 128
DN_INST_GROUP = 16
MOD_ROWS = 8
CTX_MOD_ROW = 4
VMEM_LIMIT_BYTES = 56 * 1024 * 1024
FF_CHUNKS = ((0, 1536), (1536, 1280))
TM_PREP = 512
TM_POST = 256
TM_RG = 256
TM_CTX = 256


def _cparams(n_axes):
    return pltpu.CompilerParams(dimension_semantics=("arbitrary",) * n_axes,
                                vmem_limit_bytes=VMEM_LIMIT_BYTES)


def _whole(shape):
    nd = len(shape)
    return pl.BlockSpec(shape, lambda *_: (0,) * nd, pipeline_mode=pl.Buffered(1))


def _dot(a, b):
    return jnp.dot(a, b, preferred_element_type=F32)


def _sigmoid(v):
    return 0.5 + 0.5 * jnp.tanh(0.5 * v)


def _silu(v):
    hv = 0.5 * v
    return hv + hv * jnp.tanh(hv)


F32_TINY = float(jnp.finfo(jnp.float32).tiny)
GELU_C1 = 0.7978845608028654
GELU_C2 = 0.044715 * GELU_C1


def _gelu_tanh(v):
    hv = 0.5 * v
    return hv + hv * jnp.tanh(v * (GELU_C1 + GELU_C2 * (v * v)))


def _softplus(v):
    return jnp.maximum(v, 0.0) + jnp.log1p(jnp.exp(-jnp.abs(v)))


def _log_sigmoid(v):
    return jnp.minimum(v, 0.0) - jnp.log1p(jnp.exp(-jnp.abs(v)))


def _rmsnorm(x, g):
    return x * lax.rsqrt(jnp.mean(x * x, axis=-1, keepdims=True) + NORM_EPS) * g


def _norm_mod(x, g, shift, scale):
    return _rmsnorm(x, g) * (1.0 + scale) + shift


def _mod_spec(is_ctx):
    if is_ctx:
        return pl.BlockSpec((1, MOD_ROWS, D_MODEL), lambda b, i: (CTX_MOD_ROW, 0, 0))
    return pl.BlockSpec((1, MOD_ROWS, D_MODEL), lambda b, i: (b, 0, 0))


def _tok_spec(tm, width):
    return pl.BlockSpec((1, tm, width), lambda b, i: (b, i, 0))


def _ada_kernel(act_ref, w_ref, b_ref, o_ref):
    a = _silu(act_ref[...])
    o_ref[0] = _dot(a, w_ref[0]) + b_ref[0]


def _ada_call(act, ada_w, ada_b):
    return pl.pallas_call(
        _ada_kernel,
        grid=(DEPTH, N_MOD),
        in_specs=[pl.BlockSpec((MOD_ROWS, D_MODEL), lambda l, j: (0, 0)),
                  pl.BlockSpec((1, D_MODEL, D_MODEL), lambda l, j: (l, 0, j)),
                  pl.BlockSpec((1, 1, D_MODEL), lambda l, j: (l, 0, j))],
        out_specs=pl.BlockSpec((1, MOD_ROWS, D_MODEL), lambda l, j: (l, 0, j)),
        out_shape=jax.ShapeDtypeStruct((DEPTH, MOD_ROWS, N_MOD * D_MODEL), F32),
        compiler_params=_cparams(2),
        name="ada",
    )(act, ada_w, ada_b.reshape(DEPTH, 1, N_MOD * D_MODEL))


def _ffn_residual(x, mod_ref, g_ref, wgu_ref, wd_ref, fg_ref):
    h = _norm_mod(x, g_ref[...], mod_ref[0, 3:4, :], mod_ref[0, 4:5, :]).astype(BF16)
    acc = None
    for c0, cw in FF_CHUNKS:
        gate = _dot(h, wgu_ref[:, c0:c0 + cw])
        up = _dot(h, wgu_ref[:, D_FF + c0:D_FF + c0 + cw])
        part = _dot((_silu(gate) * up).astype(BF16), wd_ref[c0:c0 + cw, :])
        acc = part if acc is None else acc + part
    y = x + mod_ref[0, 5:6, :] * acc
    if fg_ref is not None:
        y = _rmsnorm(y, fg_ref[...])
    return y


def _halo_specs(tm, t_len, halo_lo=HALO):
    per_lo = tm // halo_lo
    per_hi = tm // HALO
    last = t_len // HALO - 1
    prev = pl.BlockSpec((1, halo_lo, D_MODEL), lambda b, i: (b, jnp.maximum(i * per_lo - 1, 0), 0))
    nxt = pl.BlockSpec((1, HALO, D_MODEL), lambda b, i: (b, jnp.minimum((i + 1) * per_hi, last), 0))
    return prev, nxt


def _load_normed_with_halo(xm_ref, xp_ref, xn_ref, xcat, g, shift, scale, tm, halo_lo=HALO):
    xcat[0:halo_lo, :] = xp_ref[0]
    xcat[halo_lo:halo_lo + tm, :] = xm_ref[0]
    xcat[halo_lo + tm:halo_lo + tm + HALO, :] = xn_ref[0]
    return _norm_mod(xcat[...], g, shift, scale).astype(BF16)


def _group_iota():
    return lax.broadcasted_iota(jnp.int32, (1, SUBLANES, 1), 1)


CHUNK_GROUPS = CHUNK // SUBLANES
HALO_LO = 2 * SUBLANES


def _dwconv_interleaved(p, cw, i, nt, tm, block):
    c = p.shape[-1]
    n_chunk = tm // block
    blk_groups = block // SUBLANES
    n_grp = n_chunk * blk_groups
    p3 = p.reshape(2 + n_grp + 1, SUBLANES, c)
    lo = p3[0:2] * jnp.where(i > 0, 1.0, 0.0)
    hi = p3[n_grp + 2:n_grp + 3] * jnp.where(i < nt - 1, 1.0, 0.0)
    p3 = jnp.concatenate([lo, p3[2:n_grp + 2], hi], axis=0)
    r = _group_iota()

    def grp(k):
        return p3[k + 2:k + 3]

    def from_prev_sublane(cur, prev):
        return jnp.where(r >= 1, pltpu.roll(grp(cur), 1, 1), pltpu.roll(grp(prev), 1, 1))

    m1, m2, p1 = [], [], []
    for ch in range(n_chunk):
        b = ch * blk_groups
        wrap_m1 = from_prev_sublane(b + blk_groups - 1, b - 1)
        wrap_m2 = from_prev_sublane(b + blk_groups - 2, b - 2)
        wrap_p1 = jnp.where(r < SUBLANES - 1, pltpu.roll(grp(b), SUBLANES - 1, 1),
                            pltpu.roll(grp(b + blk_groups), SUBLANES - 1, 1))
        m1 += [wrap_m1, p3[b + 2:b + 2 + blk_groups - 1]]
        m2 += [wrap_m2, wrap_m1, p3[b + 2:b + 2 + blk_groups - 2]]
        p1 += [p3[b + 3:b + 2 + blk_groups], wrap_p1]
    w = [cw[j:j + 1, :].reshape(1, 1, c) for j in range(CONV_W)]
    out = (w[0] * jnp.concatenate(m2, axis=0) + w[1] * jnp.concatenate(m1, axis=0)
           + w[2] * p3[2:n_grp + 2] + w[3] * jnp.concatenate(p1, axis=0))
    return out.reshape(tm, c)


def _segment_scan(a, b, a_s, b_s, h_in, tm, reverse):
    n_grp = tm // SUBLANES
    a3 = a.reshape(n_grp, SUBLANES, D_MODEL)
    b3 = b.reshape(n_grp, SUBLANES, D_MODEL)
    order = range(n_grp - 1, -1, -1) if reverse else range(n_grp)
    a_run = b_run = None
    for g in order:
        if a_run is None:
            a_run, b_run = a3[g], b3[g]
        else:
            b_run = a3[g] * b_run + b3[g]
            a_run = a3[g] * a_run
        a_s[g * SUBLANES:(g + 1) * SUBLANES, :] = a_run
        b_s[g * SUBLANES:(g + 1) * SUBLANES, :] = b_run
    rows = [None] * SUBLANES
    h = h_in
    for s in (range(SUBLANES - 1, -1, -1) if reverse else range(SUBLANES)):
        rows[s] = h
        h = a_run[s:s + 1, :] * h + b_run[s:s + 1, :]
    h_runs = jnp.concatenate(rows, axis=0).reshape(1, SUBLANES, D_MODEL)
    h_all = (a_s[...].reshape(n_grp, SUBLANES, D_MODEL) * h_runs
             + b_s[...].reshape(n_grp, SUBLANES, D_MODEL))
    return h_all.reshape(tm, D_MODEL), h


def _rg_a_kernel(xm_ref, xp_ref, xn_ref, mod_ref, g_ref, wu_ref, wy_ref, bin_ref, cw_ref, cb_ref,
                 gw_ref, gb_ref, lam_ref, h0_ref,
                 hf_ref, ab_ref, bxb_ref, y_ref, hfin_ref,
                 xcat, a_s, b_s, carry, *, tm, nt):
    i = pl.program_id(1)
    h = _load_normed_with_halo(xm_ref, xp_ref, xn_ref, xcat, g_ref[...],
                               mod_ref[0, 0:1, :], mod_ref[0, 1:2, :], tm, HALO_LO)
    pu = _dot(h, wu_ref[...]) + bin_ref[:, 0:D_MODEL]
    u = _dwconv_interleaved(pu, cw_ref[...], i, nt, tm, tm) + cb_ref[...]
    py = _dot(h[HALO_LO:HALO_LO + tm, :], wy_ref[...]) + bin_ref[:, D_MODEL:2 * D_MODEL]
    y_ref[0] = _gelu_tanh(py).astype(BF16)

    ub = u.astype(BF16)
    half_u = 0.5 * u
    half_c = (0.5 * LRU_C) * _log_sigmoid(lam_ref[...])
    for d in (0, 1):
        pre = []
        for gi in (0, 1):
            parts = [_dot(ub[:, hh * LRU_BLOCK:(hh + 1) * LRU_BLOCK], gw_ref[d, gi, hh])
                     for hh in range(LRU_HEADS)]
            pre.append(jnp.concatenate(parts, axis=1) + gb_ref[d, gi:gi + 1, :])
        t_r = jnp.tanh(pre[0])
        t_i = jnp.tanh(pre[1])
        hc = half_c[d:d + 1, :]
        log_a = hc + hc * t_r
        a = jnp.exp(log_a)
        t = jnp.tanh(log_a)
        q = (-2.0 * t) / (1.0 - t)
        root = q * lax.rsqrt(jnp.maximum(q, F32_TINY))
        bx = root * (half_u + half_u * t_i)
        if d == 1:
            ab_ref[0] = a
            bxb_ref[0] = bx
        else:
            a_fwd, bx_fwd = a, bx

    @pl.when(i == 0)
    def _():
        carry[...] = h0_ref[0]

    h_all, h_end = _segment_scan(a_fwd, bx_fwd, a_s, b_s, carry[0:1, :], tm, reverse=False)
    hf_ref[0] = h_all
    carry[...] = jnp.broadcast_to(h_end, (SUBLANES, D_MODEL))

    @pl.when(i == nt - 1)
    def _():
        hfin_ref[0] = carry[...]


def _rg_a_call(x, mod, g, wu, wy, b_in, cw, cb, gw, gb, lam, h0, *, tm, is_ctx):
    bsz, t_len, _ = x.shape
    nt = t_len // tm
    prev, nxt = _halo_specs(tm, t_len, HALO_LO)
    state_spec = pl.BlockSpec((1, SUBLANES, D_MODEL), lambda b, i: (b, 0, 0))
    tok = _tok_spec(tm, D_MODEL)
    return pl.pallas_call(
        functools.partial(_rg_a_kernel, tm=tm, nt=nt),
        grid=(bsz, nt),
        in_specs=[tok, prev, nxt, _mod_spec(is_ctx), _whole((1, D_MODEL)),
                  _whole(wu.shape), _whole(wy.shape), _whole(b_in.shape), _whole(cw.shape),
                  _whole(cb.shape), _whole(gw.shape), _whole(gb.shape), _whole(lam.shape),
                  state_spec],
        out_specs=[tok, tok, tok, tok, state_spec],
        out_shape=[jax.ShapeDtypeStruct(x.shape, F32), jax.ShapeDtypeStruct(x.shape, F32),
                   jax.ShapeDtypeStruct(x.shape, F32), jax.ShapeDtypeStruct(x.shape, BF16),
                   jax.ShapeDtypeStruct((bsz, SUBLANES, D_MODEL), F32)],
        scratch_shapes=[pltpu.VMEM((tm + HALO_LO + HALO, D_MODEL), F32),
                        pltpu.VMEM((tm, D_MODEL), F32), pltpu.VMEM((tm, D_MODEL), F32),
                        pltpu.VMEM((SUBLANES, D_MODEL), F32)],
        compiler_params=_cparams(2),
        name="rg_a",
    )(x, x, x, mod, g, wu, wy, b_in, cw, cb, gw, gb, lam, h0)


def _rg_post_kernel(x_ref, hf_ref, ab_ref, bxb_ref, y_ref, h0_ref, mod_ref, wo_ref, bo_ref,
                    g2_ref, wgu_ref, wd_ref, o_ref, hfin_ref, a_s, b_s, carry, *, tm, nt):
    i = pl.program_id(1)

    @pl.when(i == 0)
    def _():
        carry[...] = h0_ref[0]

    hb, h_end = _segment_scan(ab_ref[0], bxb_ref[0], a_s, b_s, carry[0:1, :], tm, reverse=True)
    carry[...] = jnp.broadcast_to(h_end, (SUBLANES, D_MODEL))

    @pl.when(i == nt - 1)
    def _():
        hfin_ref[0] = carry[...]

    hl = hf_ref[0] + hb
    m = (hl * y_ref[0].astype(F32)).astype(BF16)
    out = _dot(m, wo_ref[...]) + bo_ref[...]
    x1 = x_ref[0] + mod_ref[0, 2:3, :] * out
    o_ref[0] = _ffn_residual(x1, mod_ref, g2_ref, wgu_ref, wd_ref, None)


def _rg_post_call(x, hf, ab, bxb, y, h0, mod, wo, bo, g2, wgu, wd, *, tm, is_ctx):
    bsz, t_len, _ = x.shape
    nt = t_len // tm
    tok = pl.BlockSpec((1, tm, D_MODEL), lambda b, i: (b, nt - 1 - i, 0))
    state_spec = pl.BlockSpec((1, SUBLANES, D_MODEL), lambda b, i: (b, 0, 0))
    return pl.pallas_call(
        functools.partial(_rg_post_kernel, tm=tm, nt=nt),
        grid=(bsz, nt),
        in_specs=[tok, tok, tok, tok, tok, state_spec, _mod_spec(is_ctx),
                  _whole(wo.shape), _whole(bo.shape), _whole((1, D_MODEL)),
                  _whole(wgu.shape), _whole(wd.shape)],
        out_specs=[tok, state_spec],
        out_shape=[jax.ShapeDtypeStruct(x.shape, F32),
                   jax.ShapeDtypeStruct((bsz, SUBLANES, D_MODEL), F32)],
        scratch_shapes=[pltpu.VMEM((tm, D_MODEL), F32), pltpu.VMEM((tm, D_MODEL), F32),
                        pltpu.VMEM((SUBLANES, D_MODEL), F32)],
        compiler_params=_cparams(2),
        name="rg_post",
    )(x, hf, ab, bxb, y, h0, mod, wo, bo, g2, wgu, wd)


def _dn_prep_kernel(xm_ref, xp_ref, xn_ref, mod_ref, g_ref, wqkv_ref, wz_ref, wba_ref, cw_ref,
                    alog_ref, dtb_ref,
                    q_ref, k_ref, v_ref, z_ref, gb_ref,
                    xcat, *, tm, nt):
    i = pl.program_id(1)
    h = _load_normed_with_halo(xm_ref, xp_ref, xn_ref, xcat, g_ref[...],
                               mod_ref[0, 0:1, :], mod_ref[0, 1:2, :], tm, HALO_LO)
    n_slab = DN_CONV_DIM // D_MODEL
    for c in range(n_slab):
        cols = slice(c * D_MODEL, (c + 1) * D_MODEL)
        hv = _dwconv_interleaved(_dot(h, wqkv_ref[:, cols]), cw_ref[:, cols], i, nt, tm, CHUNK)
        u = hv + hv * jnp.tanh(hv)
        if c < 2:
            dst = q_ref if c == 0 else k_ref
            scale = DN_HEAD ** -0.5 if c == 0 else 1.0
            for hh in range(DN_QK_HEADS):
                uh = u[:, hh * DN_HEAD:(hh + 1) * DN_HEAD]
                inv = lax.rsqrt(jnp.sum(uh * uh, axis=-1, keepdims=True) + NORM_EPS) * scale
                dst[0, :, hh * DN_HEAD:(hh + 1) * DN_HEAD] = (uh * inv).astype(BF16)
        else:
            v_ref[0, :, (c - 2) * D_MODEL:(c - 1) * D_MODEL] = u.astype(BF16)

    hm = h[HALO_LO:HALO_LO + tm, :]
    z_ref[0] = _dot(hm, wz_ref[...]).astype(BF16)
    ba = _dot(hm, wba_ref[...])
    lane = lax.broadcasted_iota(jnp.int32, (1, LANES), 1)
    beta = _sigmoid(ba)
    gdec = -jnp.exp(alog_ref[...]) * _softplus(ba + dtb_ref[...])
    n_bd = 2 * DN_V_HEADS
    gb_ref[0] = jnp.where(lane < n_bd, beta, jnp.where(lane < 2 * n_bd, gdec, 0.0))


def _dn_prep_call(x, mod, g, wqkv, wz, wba, cw, alog, dtb, *, tm, is_ctx):
    bsz, t_len, _ = x.shape
    nt = t_len // tm
    prev, nxt = _halo_specs(tm, t_len, HALO_LO)
    return pl.pallas_call(
        functools.partial(_dn_prep_kernel, tm=tm, nt=nt),
        grid=(bsz, nt),
        in_specs=[_tok_spec(tm, D_MODEL), prev, nxt, _mod_spec(is_ctx), _whole((1, D_MODEL)),
                  _whole(wqkv.shape), _whole(wz.shape), _whole(wba.shape), _whole(cw.shape),
                  _whole(alog.shape), _whole(dtb.shape)],
        out_specs=[_tok_spec(tm, DN_KEY_DIM), _tok_spec(tm, DN_KEY_DIM), _tok_spec(tm, DN_VAL_DIM),
                   _tok_spec(tm, DN_VAL_DIM), _tok_spec(tm, LANES)],
        out_shape=[jax.ShapeDtypeStruct((bsz, t_len, DN_KEY_DIM), BF16),
                   jax.ShapeDtypeStruct((bsz, t_len, DN_KEY_DIM), BF16),
                   jax.ShapeDtypeStruct((bsz, t_len, DN_VAL_DIM), BF16),
                   jax.ShapeDtypeStruct((bsz, t_len, DN_VAL_DIM), BF16),
                   jax.ShapeDtypeStruct((bsz, t_len, LANES), F32)],
        scratch_shapes=[pltpu.VMEM((tm + HALO_LO + HALO, D_MODEL), F32)],
        compiler_params=_cparams(2),
        name="dn_prep",
    )(x, x, x, mod, g, wqkv, wz, wba, cw, alog, dtb)


def _level_masks(row, col):
    masks = []
    lvl = 1
    while (1 << lvl) < CHUNK:
        same = (row >> lvl) == (col >> lvl)
        same2 = (row >> (lvl + 1)) == (col >> (lvl + 1))
        masks.append(jnp.logical_and(same2, jnp.logical_not(same)))
        lvl += 1
    return masks


def _dn_core_kernel(qf_ref, kf_ref, vf_ref, gbf_ref, qb_ref, kb_ref, vb_ref, gbb_ref,
                    s0f_ref, s0b_ref, of_ref, ob_ref, sfinf_ref, sfinb_ref, sf_s, sb_s, *, nc):
    i = pl.program_id(1)

    @pl.when(i == 0)
    def _():
        sf_s[...] = s0f_ref[0]
        sb_s[...] = s0b_ref[0]

    def token_of(phys):
        return (phys & (SUBLANES - 1)) * CHUNK_GROUPS + (phys >> 3)

    row = token_of(lax.broadcasted_iota(jnp.int32, (CHUNK, CHUNK), 0))
    col = token_of(lax.broadcasted_iota(jnp.int32, (CHUNK, CHUNK), 1))
    base_blk = (row >> 1) == (col >> 1)
    eye = jnp.where(row == col, 1.0, 0.0)
    lvl_masks = _level_masks(row, col)
    nt_dims = (((1,), (1,)), ((), ()))
    tn_dims = (((0,), (0,)), ((), ()))
    rep = DN_V_HEADS // DN_QK_HEADS

    def head_cols(hv):
        return slice(hv * DN_HEAD, (hv + 1) * DN_HEAD)

    dirs = []
    for d, (q_ref, k_ref, v_ref, gb_ref, o_ref, s_ref) in enumerate(
            ((qf_ref, kf_ref, vf_ref, gbf_ref, of_ref, sf_s),
             (qb_ref, kb_ref, vb_ref, gbb_ref, ob_ref, sb_s))):
        incl = (row <= col) if d == 1 else (row >= col)
        strict = (row < col) if d == 1 else (row > col)
        gb = gb_ref[0]
        gc_all = jnp.dot(incl.astype(F32), gb, preferred_element_type=F32,
                         precision=lax.Precision.HIGHEST)
        dirs.append(dict(q=q_ref, k=k_ref, v=v_ref, o=o_ref, s=s_ref, incl=incl, strict=strict,
                         gb=gb, gc=gc_all, gc_t=gc_all.T, last=0 if d == 1 else CHUNK - 1))

    insts = [(d, hv) for d in (0, 1) for hv in range(DN_V_HEADS)]
    for g0 in range(0, len(insts), DN_INST_GROUP):
        grp = insts[g0:g0 + DN_INST_GROUP]
        kk, qk = {}, {}
        for d, hq in sorted({(d, hv // rep) for d, hv in grp}):
            kh = dirs[d]["k"][0, :, head_cols(hq)]
            qh = dirs[d]["q"][0, :, head_cols(hq)]
            both = lax.dot_general(jnp.concatenate([kh, qh], axis=0), kh, nt_dims,
                                   preferred_element_type=F32)
            kk[d, hq] = both[0:CHUNK]
            qk[d, hq] = both[CHUNK:2 * CHUNK]
        beta_c, gc_c, mb, attn, x = {}, {}, {}, {}, {}
        for key in grp:
            d, hv = key
            dd = dirs[d]
            col_b = DN_V_HEADS * d + hv
            col_g = 2 * DN_V_HEADS + DN_V_HEADS * d + hv
            beta_c[key] = dd["gb"][:, col_b:col_b + 1]
            gc_c[key] = dd["gc"][:, col_g:col_g + 1]
            gc_r = dd["gc_t"][col_g:col_g + 1, :]
            decay = jnp.where(dd["incl"],
                              jnp.exp(jnp.where(dd["incl"], gc_c[key] - gc_r, 0.0)), 0.0)
            m = jnp.where(dd["strict"], kk[d, hv // rep] * beta_c[key] * decay, 0.0)
            x[key] = eye - jnp.where(base_blk, m, 0.0)
            mb[key] = m.astype(BF16)
            attn[key] = jnp.where(dd["incl"], qk[d, hv // rep] * decay, 0.0).astype(BF16)
        for mask in lvl_masks:
            xb = {key: x[key].astype(BF16) for key in grp}
            t = {key: _dot(xb[key], jnp.where(mask, mb[key], 0.0).astype(BF16)).astype(BF16)
                 for key in grp}
            x = {key: x[key] - _dot(t[key], xb[key]) for key in grp}
        sbf, resid = {}, {}
        for key in grp:
            d, hv = key
            dd = dirs[d]
            khf = dd["k"][0, :, head_cols(hv // rep)].astype(F32)
            kg = (khf * beta_c[key] * jnp.exp(gc_c[key])).astype(BF16)
            vb = dd["v"][0, :, head_cols(hv)].astype(F32) * beta_c[key]
            sbf[key] = dd["s"][hv].astype(BF16)
            resid[key] = (vb - _dot(kg, sbf[key])).astype(BF16)
        v_new = {key: _dot(x[key].astype(BF16), resid[key]).astype(BF16) for key in grp}
        for key in grp:
            d, hv = key
            dd = dirs[d]
            qg = (dd["q"][0, :, head_cols(hv // rep)].astype(F32)
                  * jnp.exp(gc_c[key])).astype(BF16)
            dd["o"][0, :, head_cols(hv)] = _dot(
                jnp.concatenate([qg, attn[key]], axis=1),
                jnp.concatenate([sbf[key], v_new[key]], axis=0)).astype(BF16)
        for key in grp:
            d, hv = key
            dd = dirs[d]
            g_last = gc_c[key][dd["last"]:dd["last"] + 1, :]
            khf = dd["k"][0, :, head_cols(hv // rep)].astype(F32)
            kd = (khf * jnp.exp(g_last - gc_c[key])).astype(BF16)
            dd["s"][hv] = dd["s"][hv] * jnp.exp(g_last) + lax.dot_general(
                kd, v_new[key], tn_dims, preferred_element_type=F32)

    @pl.when(i == nc - 1)
    def _():
        sfinf_ref[0] = sf_s[...]
        sfinb_ref[0] = sb_s[...]


def _dn_core_call(q, k, v, gb, s0f, s0b):
    bsz, t_len, _ = q.shape
    nc = t_len // CHUNK
    fwd = lambda b, i: (b, i, 0)
    bwd = lambda b, i: (b, nc - 1 - i, 0)
    state_spec = pl.BlockSpec((1, DN_V_HEADS, DN_HEAD, DN_HEAD), lambda b, i: (b, 0, 0, 0))
    state_shape = jax.ShapeDtypeStruct((bsz, DN_V_HEADS, DN_HEAD, DN_HEAD), F32)

    def in_specs(imap):
        return [pl.BlockSpec((1, CHUNK, DN_KEY_DIM), imap), pl.BlockSpec((1, CHUNK, DN_KEY_DIM), imap),
                pl.BlockSpec((1, CHUNK, DN_VAL_DIM), imap), pl.BlockSpec((1, CHUNK, LANES), imap)]

    return pl.pallas_call(
        functools.partial(_dn_core_kernel, nc=nc),
        grid=(bsz, nc),
        in_specs=in_specs(fwd) + in_specs(bwd) + [state_spec, state_spec],
        out_specs=[pl.BlockSpec((1, CHUNK, DN_VAL_DIM), fwd), pl.BlockSpec((1, CHUNK, DN_VAL_DIM), bwd),
                   state_spec, state_spec],
        out_shape=[jax.ShapeDtypeStruct((bsz, t_len, DN_VAL_DIM), BF16),
                   jax.ShapeDtypeStruct((bsz, t_len, DN_VAL_DIM), BF16), state_shape, state_shape],
        scratch_shapes=[pltpu.VMEM((DN_V_HEADS, DN_HEAD, DN_HEAD), F32),
                        pltpu.VMEM((DN_V_HEADS, DN_HEAD, DN_HEAD), F32)],
        compiler_params=_cparams(2),
        name="dn_core",
    )(q, k, v, gb, q, k, v, gb, s0f, s0b)


def _dn_post_kernel(of_ref, ob_ref, z_ref, x_ref, mod_ref, nw_ref, wo_ref, g2_ref, wgu_ref, wd_ref,
                    *rest, final):
    if final:
        fg_ref, o_ref = rest
    else:
        fg_ref = None
        (o_ref,) = rest
    parts = []
    for hh in range(DN_V_HEADS):
        cols = slice(hh * DN_HEAD, (hh + 1) * DN_HEAD)
        oh = of_ref[0, :, cols].astype(F32) + ob_ref[0, :, cols].astype(F32)
        oh = oh * lax.rsqrt(jnp.mean(oh * oh, axis=-1, keepdims=True) + NORM_EPS) * nw_ref[...]
        parts.append((oh * _silu(z_ref[0, :, cols].astype(F32))).astype(BF16))
    y = _dot(jnp.concatenate(parts, axis=1), wo_ref[...])
    x1 = x_ref[0] + mod_ref[0, 2:3, :] * y
    o_ref[0] = _ffn_residual(x1, mod_ref, g2_ref, wgu_ref, wd_ref, fg_ref)


def _dn_post_call(of, ob, z, x, mod, nw, wo, g2, wgu, wd, final_g, *, tm, is_ctx):
    bsz, t_len, _ = x.shape
    final = final_g is not None
    in_specs = [_tok_spec(tm, DN_VAL_DIM), _tok_spec(tm, DN_VAL_DIM), _tok_spec(tm, DN_VAL_DIM),
                _tok_spec(tm, D_MODEL), _mod_spec(is_ctx), _whole(nw.shape), _whole(wo.shape),
                _whole((1, D_MODEL)), _whole(wgu.shape), _whole(wd.shape)]
    args = [of, ob, z, x, mod, nw, wo, g2, wgu, wd]
    if final:
        in_specs.append(_whole((1, D_MODEL)))
        args.append(final_g)
    return pl.pallas_call(
        functools.partial(_dn_post_kernel, final=final),
        grid=(bsz, t_len // tm),
        in_specs=in_specs,
        out_specs=_tok_spec(tm, D_MODEL),
        out_shape=jax.ShapeDtypeStruct(x.shape, F32),
        compiler_params=_cparams(2),
        name="dn_post",
    )(*args)


def _interleave(h, block):
    b, t_len, d = h.shape
    h = h.reshape(b, t_len // block, SUBLANES, block // SUBLANES, d)
    return h.transpose(0, 1, 3, 2, 4).reshape(b, t_len, d)


def _deinterleave(h, block):
    b, t_len, d = h.shape
    h = h.reshape(b, t_len // block, block // SUBLANES, SUBLANES, d)
    return h.transpose(0, 1, 3, 2, 4).reshape(b, t_len, d)


def _rg_dn_axes(t_len):
    rows_per_tile = TM_RG // GRID_W
    w_lo = TM_RG // SUBLANES
    w_hi = GRID_W // w_lo
    assert t_len == CHUNK * GRID_W and rows_per_tile * w_hi == SUBLANES
    return (SUBLANES, CHUNK_GROUPS // rows_per_tile, w_lo, rows_per_tile, w_hi)


def _rg_to_dn(h):
    b, t_len, d = h.shape
    h = h.reshape(b, *_rg_dn_axes(t_len), d)
    return h.transpose(0, 5, 3, 2, 4, 1, 6).reshape(b, t_len, d)


def _dn_to_rg(h):
    b, t_len, d = h.shape
    r_hi, t_lo, w_lo, r4, w_hi = _rg_dn_axes(t_len)
    h = h.reshape(b, w_hi, w_lo, t_lo, r4, r_hi, d)
    return h.transpose(0, 5, 3, 2, 4, 1, 6).reshape(b, t_len, d)


def _dn_to_row_major(h):
    b, t_len, d = h.shape
    h = h.reshape(b, GRID_W, CHUNK_GROUPS, SUBLANES, d)
    return h.transpose(0, 3, 2, 1, 4).reshape(b, t_len, d)


def _pad_lanes(a, width=LANES):
    return jnp.pad(a, [(0, 0)] * (a.ndim - 1) + [(0, width - a.shape[-1])])


def kernel(x, c, ctx, c_ctx, ada_w, ada_b, norm_g, final_norm_g, rg_w_in, rg_b_in, rg_conv_w,
           rg_conv_b, rg_gate_w, rg_gate_b, rg_lambda, rg_w_out, rg_b_out, dn_w_in, dn_conv_w,
           dn_a_log, dn_dt_bias, dn_norm_w, dn_w_out, ffn_w_gu, ffn_w_down):
    bsz = x.shape[0]
    act = jnp.zeros((MOD_ROWS, D_MODEL), F32).at[0:bsz].set(c).at[CTX_MOD_ROW].set(c_ctx)
    mods = _ada_call(act, ada_w, ada_b).reshape(DEPTH, MOD_ROWS, N_MOD, D_MODEL)
    mods = jnp.pad(mods, ((0, 0), (0, 0), (0, MOD_ROWS - N_MOD), (0, 0)))
    zero_state = jnp.zeros((bsz, SUBLANES, D_MODEL), F32)
    zero_s = jnp.zeros((bsz, DN_V_HEADS, DN_HEAD, DN_HEAD), F32)
    x = _interleave(x, TM_RG)

    for layer in range(DEPTH):
        last = layer == DEPTH - 1
        mod = mods[layer]
        g1 = norm_g[layer, 0].reshape(1, D_MODEL)
        g2 = norm_g[layer, 1].reshape(1, D_MODEL)
        wgu = ffn_w_gu[layer].astype(BF16)
        wd = ffn_w_down[layer].astype(BF16)
        j = layer // 2
        if layer % 2 == 0:
            wu = rg_w_in[j][:, :D_MODEL].astype(BF16)
            wy = rg_w_in[j][:, D_MODEL:].astype(BF16)
            b_in = rg_b_in[j].reshape(1, 2 * D_MODEL)
            cw = rg_conv_w[j]
            cb = rg_conv_b[j].reshape(1, D_MODEL)
            gw = (0.5 * rg_gate_w[j]).astype(BF16)
            gbias = 0.5 * rg_gate_b[j]
            lam = rg_lambda[j]
            wo = rg_w_out[j].astype(BF16)
            bo = rg_b_out[j].reshape(1, D_MODEL)

            def layer_fn(xs, h0f, h0b, tm_a, tm_b, is_ctx):
                hf, ab, bxb, y, hf_fin = _rg_a_call(xs, mod, g1, wu, wy, b_in, cw, cb, gw, gbias,
                                                    lam, h0f, tm=tm_a, is_ctx=is_ctx)
                out, hb_fin = _rg_post_call(xs, hf, ab, bxb, y, h0b, mod, wo, bo, g2, wgu, wd,
                                            tm=tm_b, is_ctx=is_ctx)
                return out, hf_fin, hb_fin

            ctx_new, hf_c, hb_c = layer_fn(_interleave(ctx, TM_RG), zero_state, zero_state,
                                           TM_RG, TM_RG, True)
            x, _, _ = layer_fn(x, hf_c, hb_c, TM_RG, TM_RG, False)
            ctx = _deinterleave(ctx_new, TM_RG)
        else:
            w_in = dn_w_in[j]
            wqkv = w_in[:, :DN_CONV_DIM].astype(BF16)
            wz = w_in[:, DN_CONV_DIM:DN_CONV_DIM + DN_VAL_DIM].astype(BF16)
            wba = _pad_lanes(w_in[:, DN_CONV_DIM + DN_VAL_DIM:]).astype(BF16)
            cw = 0.5 * dn_conv_w[j]
            n_bd = 2 * DN_V_HEADS
            alog = jnp.zeros((1, LANES), F32).at[0, n_bd:2 * n_bd].set(dn_a_log[j].reshape(-1))
            dtb = jnp.zeros((1, LANES), F32).at[0, n_bd:2 * n_bd].set(dn_dt_bias[j].reshape(-1))
            nw = dn_norm_w[j].reshape(1, DN_HEAD)
            wo = dn_w_out[j].astype(BF16)
            final_g = final_norm_g.reshape(1, D_MODEL) if last else None

            def prep(xs, tm, is_ctx):
                return _dn_prep_call(xs, mod, g1, wqkv, wz, wba, cw, alog, dtb, tm=tm, is_ctx=is_ctx)

            ctx_il = _interleave(ctx, CHUNK)
            qc, kc, vc, zc, gbc = prep(ctx_il, TM_CTX, True)
            ofc, obc, sf, sb = _dn_core_call(qc, kc, vc, gbc, zero_s, zero_s)
            x_cm = _rg_to_dn(x)
            q, k, v, z, gb = prep(x_cm, TM_PREP, False)
            of, ob, _, _ = _dn_core_call(q, k, v, gb, sf, sb)
            x_cm = _dn_post_call(of, ob, z, x_cm, mod, nw, wo, g2, wgu, wd, final_g,
                                 tm=TM_POST, is_ctx=False)
            if last:
                return _dn_to_row_major(x_cm)
            x = _dn_to_rg(x_cm)
            ctx = _deinterleave(
                _dn_post_call(ofc, obc, zc, ctx_il, mod, nw, wo, g2, wgu, wd, None,
                              tm=TM_CTX, is_ctx=True), CHUNK)
    raise AssertionError("the last layer is a DeltaNet layer and returns above")
```

```python
import functools

import jax
import jax.numpy as jnp
from jax import lax
from jax.experimental import pallas as pl
from jax.experimental.pallas import tpu as pltpu

F32 = jnp.float32
BF16 = jnp.bfloat16

D_MODEL = 1024
DEPTH = 4
GRID_W = 64
N_MOD = 6
NORM_EPS = 1e-6
CONV_W = 4
LRU_HEADS = 4
LRU_BLOCK = D_MODEL // LRU_HEADS
LRU_C = 8.0
DN_QK_HEADS = 8
DN_V_HEADS = 16
DN_HEAD = 128
DN_KEY_DIM = DN_QK_HEADS * DN_HEAD
DN_VAL_DIM = DN_V_HEADS * DN_HEAD
DN_CONV_DIM = 2 * DN_KEY_DIM + DN_VAL_DIM
D_FF = 2816

SUBLANES = 8
LANES = 128
HALO = SUBLANES
CHUNK = 128
DN_INST_GROUP = 16
MOD_ROWS = 8
CTX_MOD_ROW = 4
VMEM_LIMIT_BYTES = 56 * 1024 * 1024
FF_CHUNKS = ((0, 1536), (1536, 1280))
TM_PREP = 512
TM_POST = 256
TM_CTX = 256


def _cparams(n_axes):
    return pltpu.CompilerParams(dimension_semantics=("arbitrary",) * n_axes,
                                vmem_limit_bytes=VMEM_LIMIT_BYTES)


def _whole(shape):
    nd = len(shape)
    return pl.BlockSpec(shape, lambda *_: (0,) * nd, pipeline_mode=pl.Buffered(1))


def _col_block(rows, width, j):
    return pl.BlockSpec((rows, width), lambda *_: (0, j), pipeline_mode=pl.Buffered(1))


def _dot(a, b):
    return jnp.dot(a, b, preferred_element_type=F32)


def _sigmoid(v):
    return 0.5 + 0.5 * jnp.tanh(0.5 * v)


def _silu(v):
    hv = 0.5 * v
    return hv + hv * jnp.tanh(hv)


F32_TINY = float(jnp.finfo(jnp.float32).tiny)
GELU_C1 = 0.7978845608028654
GELU_C2 = 0.044715 * GELU_C1


def _gelu_tanh(v):
    hv = 0.5 * v
    return hv + hv * jnp.tanh(v * (GELU_C1 + GELU_C2 * (v * v)))


def _softplus(v):
    return jnp.maximum(v, 0.0) + jnp.log1p(jnp.exp(-jnp.abs(v)))


def _log_sigmoid(v):
    return jnp.minimum(v, 0.0) - jnp.log1p(jnp.exp(-jnp.abs(v)))


def _rmsnorm(x, g):
    return x * lax.rsqrt(jnp.mean(x * x, axis=-1, keepdims=True) + NORM_EPS) * g


def _norm_mod(x, g, shift, scale):
    return _rmsnorm(x, g) * (1.0 + scale) + shift


def _mod_spec(is_ctx):
    if is_ctx:
        return pl.BlockSpec((1, MOD_ROWS, D_MODEL), lambda b, i: (CTX_MOD_ROW, 0, 0))
    return pl.BlockSpec((1, MOD_ROWS, D_MODEL), lambda b, i: (b, 0, 0))


def _tok_spec(tm, width):
    return pl.BlockSpec((1, tm, width), lambda b, i: (b, i, 0))


def _ada_kernel(act_ref, w_ref, b_ref, o_ref):
    a = _silu(act_ref[...])
    o_ref[0] = _dot(a, w_ref[0]) + b_ref[0]


def _ada_call(act, ada_w, ada_b):
    return pl.pallas_call(
        _ada_kernel,
        grid=(DEPTH, N_MOD),
        in_specs=[pl.BlockSpec((MOD_ROWS, D_MODEL), lambda l, j: (0, 0)),
                  pl.BlockSpec((1, D_MODEL, D_MODEL), lambda l, j: (l, 0, j)),
                  pl.BlockSpec((1, 1, D_MODEL), lambda l, j: (l, 0, j))],
        out_specs=pl.BlockSpec((1, MOD_ROWS, D_MODEL), lambda l, j: (l, 0, j)),
        out_shape=jax.ShapeDtypeStruct((DEPTH, MOD_ROWS, N_MOD * D_MODEL), F32),
        compiler_params=_cparams(2),
        name="ada",
    )(act, ada_w, ada_b.reshape(DEPTH, 1, N_MOD * D_MODEL))


def _ffn_residual(x, mod_ref, g_ref, wgu_ref, wd_ref, fg_ref):
    h = _norm_mod(x, g_ref[...], mod_ref[0, 3:4, :], mod_ref[0, 4:5, :]).astype(BF16)
    acc = None
    for c0, cw in FF_CHUNKS:
        gate = _dot(h, wgu_ref[:, c0:c0 + cw])
        up = _dot(h, wgu_ref[:, D_FF + c0:D_FF + c0 + cw])
        part = _dot((_silu(gate) * up).astype(BF16), wd_ref[c0:c0 + cw, :])
        acc = part if acc is None else acc + part
    y = x + mod_ref[0, 5:6, :] * acc
    if fg_ref is not None:
        y = _rmsnorm(y, fg_ref[...])
    return y


def _halo_specs(tm, t_len, halo_lo=HALO):
    per_lo = tm // halo_lo
    per_hi = tm // HALO
    last = t_len // HALO - 1
    prev = pl.BlockSpec((1, halo_lo, D_MODEL), lambda b, i: (b, jnp.maximum(i * per_lo - 1, 0), 0))
    nxt = pl.BlockSpec((1, HALO, D_MODEL), lambda b, i: (b, jnp.minimum((i + 1) * per_hi, last), 0))
    return prev, nxt


def _load_normed_with_halo(xm_ref, xp_ref, xn_ref, xcat, g, shift, scale, tm, halo_lo=HALO):
    xcat[0:halo_lo, :] = xp_ref[0]
    xcat[halo_lo:halo_lo + tm, :] = xm_ref[0]
    xcat[halo_lo + tm:halo_lo + tm + HALO, :] = xn_ref[0]
    return _norm_mod(xcat[...], g, shift, scale).astype(BF16)


def _group_iota():
    return lax.broadcasted_iota(jnp.int32, (1, SUBLANES, 1), 1)


def _dwconv(p, cw, i, nt, tm):
    n_grp = tm // SUBLANES
    c = p.shape[-1]
    p3 = p.reshape(n_grp + 2, SUBLANES, c)
    lo = p3[0:1] * jnp.where(i > 0, 1.0, 0.0)
    hi = p3[n_grp + 1:n_grp + 2] * jnp.where(i < nt - 1, 1.0, 0.0)
    p3 = jnp.concatenate([lo, p3[1:n_grp + 1], hi], axis=0)
    r = _group_iota()
    r2 = pltpu.roll(p3, 2, 1)
    r1 = pltpu.roll(p3, 1, 1)
    r7 = pltpu.roll(p3, SUBLANES - 1, 1)
    tap_m2 = jnp.where(r >= 2, r2[1:n_grp + 1], r2[0:n_grp])
    tap_m1 = jnp.where(r >= 1, r1[1:n_grp + 1], r1[0:n_grp])
    tap_p1 = jnp.where(r < SUBLANES - 1, r7[1:n_grp + 1], r7[2:n_grp + 2])
    w = [cw[j:j + 1, :].reshape(1, 1, c) for j in range(CONV_W)]
    out = w[0] * tap_m2 + w[1] * tap_m1 + w[2] * p3[1:n_grp + 1] + w[3] * tap_p1
    return out.reshape(tm, c)


CHUNK_GROUPS = CHUNK // SUBLANES
DN_HALO_LO = 2 * SUBLANES


def _dwconv_interleaved(p, cw, i, nt, tm):
    c = p.shape[-1]
    n_chunk = tm // CHUNK
    n_grp = n_chunk * CHUNK_GROUPS
    p3 = p.reshape(2 + n_grp + 1, SUBLANES, c)
    lo = p3[0:2] * jnp.where(i > 0, 1.0, 0.0)
    hi = p3[n_grp + 2:n_grp + 3] * jnp.where(i < nt - 1, 1.0, 0.0)
    p3 = jnp.concatenate([lo, p3[2:n_grp + 2], hi], axis=0)
    r = _group_iota()

    def grp(k):
        return p3[k + 2:k + 3]

    def from_prev_sublane(cur, prev):
        return jnp.where(r >= 1, pltpu.roll(grp(cur), 1, 1), pltpu.roll(grp(prev), 1, 1))

    m1, m2, p1 = [], [], []
    for ch in range(n_chunk):
        b = ch * CHUNK_GROUPS
        wrap_m1 = from_prev_sublane(b + CHUNK_GROUPS - 1, b - 1)
        wrap_m2 = from_prev_sublane(b + CHUNK_GROUPS - 2, b - 2)
        wrap_p1 = jnp.where(r < SUBLANES - 1, pltpu.roll(grp(b), SUBLANES - 1, 1),
                            pltpu.roll(grp(b + CHUNK_GROUPS), SUBLANES - 1, 1))
        m1 += [wrap_m1, p3[b + 2:b + 2 + CHUNK_GROUPS - 1]]
        m2 += [wrap_m2, wrap_m1, p3[b + 2:b + 2 + CHUNK_GROUPS - 2]]
        p1 += [p3[b + 3:b + 2 + CHUNK_GROUPS], wrap_p1]
    w = [cw[j:j + 1, :].reshape(1, 1, c) for j in range(CONV_W)]
    out = (w[0] * jnp.concatenate(m2, axis=0) + w[1] * jnp.concatenate(m1, axis=0)
           + w[2] * p3[2:n_grp + 2] + w[3] * jnp.concatenate(p1, axis=0))
    return out.reshape(tm, c)


def _local_scan(a, b, tm, reverse):
    a3 = a.reshape(tm // SUBLANES, SUBLANES, D_MODEL)
    b3 = b.reshape(tm // SUBLANES, SUBLANES, D_MODEL)
    r = _group_iota()
    for s in (1, 2, 4):
        shift = SUBLANES - s if reverse else s
        a_sh = pltpu.roll(a3, shift, 1)
        b_sh = pltpu.roll(b3, shift, 1)
        valid = (r < SUBLANES - s) if reverse else (r >= s)
        b3 = jnp.where(valid, a3 * b_sh + b3, b3)
        a3 = jnp.where(valid, a3 * a_sh, a3)
    return a3.reshape(tm, D_MODEL), b3.reshape(tm, D_MODEL)


def _carry_scan(a_s, b_s, write, h, tm, reverse):
    n_grp = tm // SUBLANES

    def body(t, h):
        g = (n_grp - 1 - t) if reverse else t
        r0 = pl.multiple_of(g * SUBLANES, SUBLANES)
        hh = a_s[pl.ds(r0, SUBLANES), :] * h + b_s[pl.ds(r0, SUBLANES), :]
        write(r0, hh)
        return hh[0:1, :] if reverse else hh[SUBLANES - 1:SUBLANES, :]

    return lax.fori_loop(0, n_grp, body, h, unroll=8)


def _rg_a_kernel(xm_ref, xp_ref, xn_ref, mod_ref, g_ref, wu_ref, wy_ref, bin_ref, cw_ref, cb_ref,
                 gw_ref, gb_ref, lam_ref, h0_ref,
                 hf_ref, ab_ref, bxb_ref, y_ref, hfin_ref,
                 xcat, a_s, b_s, carry, *, tm, nt):
    i = pl.program_id(1)
    h = _load_normed_with_halo(xm_ref, xp_ref, xn_ref, xcat, g_ref[...],
                               mod_ref[0, 0:1, :], mod_ref[0, 1:2, :], tm)
    pu = _dot(h, wu_ref[...]) + bin_ref[:, 0:D_MODEL]
    u = _dwconv(pu, cw_ref[...], i, nt, tm) + cb_ref[...]
    py = _dot(h[HALO:HALO + tm, :], wy_ref[...]) + bin_ref[:, D_MODEL:2 * D_MODEL]
    y_ref[0] = _gelu_tanh(py).astype(BF16)

    ub = u.astype(BF16)
    half_u = 0.5 * u
    half_c = (0.5 * LRU_C) * _log_sigmoid(lam_ref[...])
    for d in (0, 1):
        pre = []
        for gi in (0, 1):
            parts = [_dot(ub[:, hh * LRU_BLOCK:(hh + 1) * LRU_BLOCK], gw_ref[d, gi, hh])
                     for hh in range(LRU_HEADS)]
            pre.append(jnp.concatenate(parts, axis=1) + gb_ref[d, gi:gi + 1, :])
        t_r = jnp.tanh(pre[0])
        t_i = jnp.tanh(pre[1])
        hc = half_c[d:d + 1, :]
        log_a = hc + hc * t_r
        a = jnp.exp(log_a)
        t = jnp.tanh(log_a)
        q = (-2.0 * t) / (1.0 - t)
        root = q * lax.rsqrt(jnp.maximum(q, F32_TINY))
        bx = root * (half_u + half_u * t_i)
        if d == 1:
            ab_ref[0] = a
            bxb_ref[0] = bx
        else:
            a_loc, b_loc = _local_scan(a, bx, tm, reverse=False)
            a_s[...] = a_loc
            b_s[...] = b_loc

    @pl.when(i == 0)
    def _():
        carry[...] = h0_ref[0]

    def write(r0, hh):
        hf_ref[0, pl.ds(r0, SUBLANES), :] = hh

    h_end = _carry_scan(a_s, b_s, write, carry[0:1, :], tm, reverse=False)
    carry[...] = jnp.broadcast_to(h_end, (SUBLANES, D_MODEL))

    @pl.when(i == nt - 1)
    def _():
        hfin_ref[0] = carry[...]


def _rg_a_call(x, mod, g, w_in, b_in, cw, cb, gw, gb, lam, h0, *, tm, is_ctx):
    bsz, t_len, _ = x.shape
    nt = t_len // tm
    prev, nxt = _halo_specs(tm, t_len)
    state_spec = pl.BlockSpec((1, SUBLANES, D_MODEL), lambda b, i: (b, 0, 0))
    tok = _tok_spec(tm, D_MODEL)
    return pl.pallas_call(
        functools.partial(_rg_a_kernel, tm=tm, nt=nt),
        grid=(bsz, nt),
        in_specs=[tok, prev, nxt, _mod_spec(is_ctx), _whole((1, D_MODEL)),
                  _col_block(D_MODEL, D_MODEL, 0), _col_block(D_MODEL, D_MODEL, 1),
                  _whole(b_in.shape), _whole(cw.shape),
                  _whole(cb.shape), _whole(gw.shape), _whole(gb.shape), _whole(lam.shape),
                  state_spec],
        out_specs=[tok, tok, tok, tok, state_spec],
        out_shape=[jax.ShapeDtypeStruct(x.shape, F32), jax.ShapeDtypeStruct(x.shape, F32),
                   jax.ShapeDtypeStruct(x.shape, F32), jax.ShapeDtypeStruct(x.shape, BF16),
                   jax.ShapeDtypeStruct((bsz, SUBLANES, D_MODEL), F32)],
        scratch_shapes=[pltpu.VMEM((tm + 2 * HALO, D_MODEL), F32),
                        pltpu.VMEM((tm, D_MODEL), F32), pltpu.VMEM((tm, D_MODEL), F32),
                        pltpu.VMEM((SUBLANES, D_MODEL), F32)],
        compiler_params=_cparams(2),
        name="rg_a",
    )(x, x, x, mod, g, w_in, w_in, b_in, cw, cb, gw, gb, lam, h0)


def _rg_post_kernel(x_ref, hf_ref, ab_ref, bxb_ref, y_ref, h0_ref, mod_ref, wo_ref, bo_ref,
                    g2_ref, wgu_ref, wd_ref, o_ref, hfin_ref, a_s, b_s, carry, *, tm, nt):
    i = pl.program_id(1)
    a_loc, b_loc = _local_scan(ab_ref[0], bxb_ref[0], tm, reverse=True)
    a_s[...] = a_loc
    b_s[...] = b_loc

    @pl.when(i == 0)
    def _():
        carry[...] = h0_ref[0]

    def write(r0, hh):
        b_s[pl.ds(r0, SUBLANES), :] = hh

    h_end = _carry_scan(a_s, b_s, write, carry[0:1, :], tm, reverse=True)
    carry[...] = jnp.broadcast_to(h_end, (SUBLANES, D_MODEL))

    @pl.when(i == nt - 1)
    def _():
        hfin_ref[0] = carry[...]

    hl = hf_ref[0] + b_s[...]
    m = (hl * y_ref[0].astype(F32)).astype(BF16)
    out = _dot(m, wo_ref[...]) + bo_ref[...]
    x1 = x_ref[0] + mod_ref[0, 2:3, :] * out
    o_ref[0] = _ffn_residual(x1, mod_ref, g2_ref, wgu_ref, wd_ref, None)


def _rg_post_call(x, hf, ab, bxb, y, h0, mod, wo, bo, g2, wgu, wd, *, tm, is_ctx):
    bsz, t_len, _ = x.shape
    nt = t_len // tm
    tok = pl.BlockSpec((1, tm, D_MODEL), lambda b, i: (b, nt - 1 - i, 0))
    state_spec = pl.BlockSpec((1, SUBLANES, D_MODEL), lambda b, i: (b, 0, 0))
    return pl.pallas_call(
        functools.partial(_rg_post_kernel, tm=tm, nt=nt),
        grid=(bsz, nt),
        in_specs=[tok, tok, tok, tok, tok, state_spec, _mod_spec(is_ctx),
                  _whole(wo.shape), _whole(bo.shape), _whole((1, D_MODEL)),
                  _whole(wgu.shape), _whole(wd.shape)],
        out_specs=[tok, state_spec],
        out_shape=[jax.ShapeDtypeStruct(x.shape, F32),
                   jax.ShapeDtypeStruct((bsz, SUBLANES, D_MODEL), F32)],
        scratch_shapes=[pltpu.VMEM((tm, D_MODEL), F32), pltpu.VMEM((tm, D_MODEL), F32),
                        pltpu.VMEM((SUBLANES, D_MODEL), F32)],
        compiler_params=_cparams(2),
        name="rg_post",
    )(x, hf, ab, bxb, y, h0, mod, wo, bo, g2, wgu, wd)


def _dn_prep_kernel(xm_ref, xp_ref, xn_ref, mod_ref, g_ref, wqkv_ref, wz_ref, wba_ref, cw_ref,
                    alog_ref, dtb_ref,
                    q_ref, k_ref, v_ref, z_ref, gb_ref,
                    xcat, *, tm, nt):
    i = pl.program_id(1)
    h = _load_normed_with_halo(xm_ref, xp_ref, xn_ref, xcat, g_ref[...],
                               mod_ref[0, 0:1, :], mod_ref[0, 1:2, :], tm, DN_HALO_LO)
    n_slab = DN_CONV_DIM // D_MODEL
    for c in range(n_slab):
        cols = slice(c * D_MODEL, (c + 1) * D_MODEL)
        hv = _dwconv_interleaved(_dot(h, wqkv_ref[:, cols]), cw_ref[:, cols], i, nt, tm)
        u = hv + hv * jnp.tanh(hv)
        if c < 2:
            dst = q_ref if c == 0 else k_ref
            scale = DN_HEAD ** -0.5 if c == 0 else 1.0
            for hh in range(DN_QK_HEADS):
                uh = u[:, hh * DN_HEAD:(hh + 1) * DN_HEAD]
                inv = lax.rsqrt(jnp.sum(uh * uh, axis=-1, keepdims=True) + NORM_EPS) * scale
                dst[0, :, hh * DN_HEAD:(hh + 1) * DN_HEAD] = (uh * inv).astype(BF16)
        else:
            v_ref[0, :, (c - 2) * D_MODEL:(c - 1) * D_MODEL] = u.astype(BF16)

    hm = h[DN_HALO_LO:DN_HALO_LO + tm, :]
    z_ref[0] = _dot(hm, wz_ref[...]).astype(BF16)
    ba = _dot(hm, wba_ref[...])
    lane = lax.broadcasted_iota(jnp.int32, (1, LANES), 1)
    beta = _sigmoid(ba)
    gdec = -jnp.exp(alog_ref[...]) * _softplus(ba + dtb_ref[...])
    n_bd = 2 * DN_V_HEADS
    gb_ref[0] = jnp.where(lane < n_bd, beta, jnp.where(lane < 2 * n_bd, gdec, 0.0))


def _dn_prep_call(x, mod, g, w_in, wba, cw, alog, dtb, *, tm, is_ctx):
    bsz, t_len, _ = x.shape
    nt = t_len // tm
    prev, nxt = _halo_specs(tm, t_len, DN_HALO_LO)
    return pl.pallas_call(
        functools.partial(_dn_prep_kernel, tm=tm, nt=nt),
        grid=(bsz, nt),
        in_specs=[_tok_spec(tm, D_MODEL), prev, nxt, _mod_spec(is_ctx), _whole((1, D_MODEL)),
                  _col_block(D_MODEL, DN_CONV_DIM, 0),
                  _col_block(D_MODEL, DN_VAL_DIM, DN_CONV_DIM // DN_VAL_DIM),
                  _whole(wba.shape), _whole(cw.shape),
                  _whole(alog.shape), _whole(dtb.shape)],
        out_specs=[_tok_spec(tm, DN_KEY_DIM), _tok_spec(tm, DN_KEY_DIM), _tok_spec(tm, DN_VAL_DIM),
                   _tok_spec(tm, DN_VAL_DIM), _tok_spec(tm, LANES)],
        out_shape=[jax.ShapeDtypeStruct((bsz, t_len, DN_KEY_DIM), BF16),
                   jax.ShapeDtypeStruct((bsz, t_len, DN_KEY_DIM), BF16),
                   jax.ShapeDtypeStruct((bsz, t_len, DN_VAL_DIM), BF16),
                   jax.ShapeDtypeStruct((bsz, t_len, DN_VAL_DIM), BF16),
                   jax.ShapeDtypeStruct((bsz, t_len, LANES), F32)],
        scratch_shapes=[pltpu.VMEM((tm + DN_HALO_LO + HALO, D_MODEL), F32)],
        compiler_params=_cparams(2),
        name="dn_prep",
    )(x, x, x, mod, g, w_in, w_in, wba, cw, alog, dtb)


def _level_masks(row, col):
    masks = []
    lvl = 1
    while (1 << lvl) < CHUNK:
        same = (row >> lvl) == (col >> lvl)
        same2 = (row >> (lvl + 1)) == (col >> (lvl + 1))
        masks.append(jnp.logical_and(same2, jnp.logical_not(same)))
        lvl += 1
    return masks


def _dn_core_kernel(qf_ref, kf_ref, vf_ref, gbf_ref, qb_ref, kb_ref, vb_ref, gbb_ref,
                    s0f_ref, s0b_ref, of_ref, ob_ref, sfinf_ref, sfinb_ref, sf_s, sb_s, *, nc):
    i = pl.program_id(1)

    @pl.when(i == 0)
    def _():
        sf_s[...] = s0f_ref[0]
        sb_s[...] = s0b_ref[0]

    def token_of(phys):
        return (phys & (SUBLANES - 1)) * CHUNK_GROUPS + (phys >> 3)

    row = token_of(lax.broadcasted_iota(jnp.int32, (CHUNK, CHUNK), 0))
    col = token_of(lax.broadcasted_iota(jnp.int32, (CHUNK, CHUNK), 1))
    base_blk = (row >> 1) == (col >> 1)
    eye = jnp.where(row == col, 1.0, 0.0)
    lvl_masks = _level_masks(row, col)
    nt_dims = (((1,), (1,)), ((), ()))
    tn_dims = (((0,), (0,)), ((), ()))
    rep = DN_V_HEADS // DN_QK_HEADS

    def head_cols(hv):
        return slice(hv * DN_HEAD, (hv + 1) * DN_HEAD)

    dirs = []
    for d, (q_ref, k_ref, v_ref, gb_ref, o_ref, s_ref) in enumerate(
            ((qf_ref, kf_ref, vf_ref, gbf_ref, of_ref, sf_s),
             (qb_ref, kb_ref, vb_ref, gbb_ref, ob_ref, sb_s))):
        incl = (row <= col) if d == 1 else (row >= col)
        strict = (row < col) if d == 1 else (row > col)
        gb = gb_ref[0]
        gc_all = jnp.dot(incl.astype(F32), gb, preferred_element_type=F32,
                         precision=lax.Precision.HIGHEST)
        dirs.append(dict(q=q_ref, k=k_ref, v=v_ref, o=o_ref, s=s_ref, incl=incl, strict=strict,
                         gb=gb, gc=gc_all, gc_t=gc_all.T, last=0 if d == 1 else CHUNK - 1))

    insts = [(d, hv) for d in (0, 1) for hv in range(DN_V_HEADS)]
    for g0 in range(0, len(insts), DN_INST_GROUP):
        grp = insts[g0:g0 + DN_INST_GROUP]
        kk, qk = {}, {}
        for d, hq in sorted({(d, hv // rep) for d, hv in grp}):
            kh = dirs[d]["k"][0, :, head_cols(hq)]
            qh = dirs[d]["q"][0, :, head_cols(hq)]
            both = lax.dot_general(jnp.concatenate([kh, qh], axis=0), kh, nt_dims,
                                   preferred_element_type=F32)
            kk[d, hq] = both[0:CHUNK]
            qk[d, hq] = both[CHUNK:2 * CHUNK]
        beta_c, gc_c, mb, attn, x = {}, {}, {}, {}, {}
        for key in grp:
            d, hv = key
            dd = dirs[d]
            col_b = DN_V_HEADS * d + hv
            col_g = 2 * DN_V_HEADS + DN_V_HEADS * d + hv
            beta_c[key] = dd["gb"][:, col_b:col_b + 1]
            gc_c[key] = dd["gc"][:, col_g:col_g + 1]
            gc_r = dd["gc_t"][col_g:col_g + 1, :]
            decay = jnp.where(dd["incl"],
                              jnp.exp(jnp.where(dd["incl"], gc_c[key] - gc_r, 0.0)), 0.0)
            m = jnp.where(dd["strict"], kk[d, hv // rep] * beta_c[key] * decay, 0.0)
            x[key] = eye - jnp.where(base_blk, m, 0.0)
            mb[key] = m.astype(BF16)
            attn[key] = jnp.where(dd["incl"], qk[d, hv // rep] * decay, 0.0).astype(BF16)
        for mask in lvl_masks:
            xb = {key: x[key].astype(BF16) for key in grp}
            t = {key: _dot(xb[key], jnp.where(mask, mb[key], 0.0).astype(BF16)).astype(BF16)
                 for key in grp}
            x = {key: x[key] - _dot(t[key], xb[key]) for key in grp}
        sbf, resid = {}, {}
        for key in grp:
            d, hv = key
            dd = dirs[d]
            khf = dd["k"][0, :, head_cols(hv // rep)].astype(F32)
            kg = (khf * beta_c[key] * jnp.exp(gc_c[key])).astype(BF16)
            vb = dd["v"][0, :, head_cols(hv)].astype(F32) * beta_c[key]
            sbf[key] = dd["s"][hv].astype(BF16)
            resid[key] = (vb - _dot(kg, sbf[key])).astype(BF16)
        v_new = {key: _dot(x[key].astype(BF16), resid[key]).astype(BF16) for key in grp}
        for key in grp:
            d, hv = key
            dd = dirs[d]
            qg = (dd["q"][0, :, head_cols(hv // rep)].astype(F32)
                  * jnp.exp(gc_c[key])).astype(BF16)
            dd["o"][0, :, head_cols(hv)] = _dot(
                jnp.concatenate([qg, attn[key]], axis=1),
                jnp.concatenate([sbf[key], v_new[key]], axis=0)).astype(BF16)
        for key in grp:
            d, hv = key
            dd = dirs[d]
            g_last = gc_c[key][dd["last"]:dd["last"] + 1, :]
            khf = dd["k"][0, :, head_cols(hv // rep)].astype(F32)
            kd = (khf * jnp.exp(g_last - gc_c[key])).astype(BF16)
            dd["s"][hv] = dd["s"][hv] * jnp.exp(g_last) + lax.dot_general(
                kd, v_new[key], tn_dims, preferred_element_type=F32)

    @pl.when(i == nc - 1)
    def _():
        sfinf_ref[0] = sf_s[...]
        sfinb_ref[0] = sb_s[...]


def _dn_core_call(q, k, v, gb, s0f, s0b):
    bsz, t_len, _ = q.shape
    nc = t_len // CHUNK
    fwd = lambda b, i: (b, i, 0)
    bwd = lambda b, i: (b, nc - 1 - i, 0)
    state_spec = pl.BlockSpec((1, DN_V_HEADS, DN_HEAD, DN_HEAD), lambda b, i: (b, 0, 0, 0))
    state_shape = jax.ShapeDtypeStruct((bsz, DN_V_HEADS, DN_HEAD, DN_HEAD), F32)

    def in_specs(imap):
        return [pl.BlockSpec((1, CHUNK, DN_KEY_DIM), imap), pl.BlockSpec((1, CHUNK, DN_KEY_DIM), imap),
                pl.BlockSpec((1, CHUNK, DN_VAL_DIM), imap), pl.BlockSpec((1, CHUNK, LANES), imap)]

    return pl.pallas_call(
        functools.partial(_dn_core_kernel, nc=nc),
        grid=(bsz, nc),
        in_specs=in_specs(fwd) + in_specs(bwd) + [state_spec, state_spec],
        out_specs=[pl.BlockSpec((1, CHUNK, DN_VAL_DIM), fwd), pl.BlockSpec((1, CHUNK, DN_VAL_DIM), bwd),
                   state_spec, state_spec],
        out_shape=[jax.ShapeDtypeStruct((bsz, t_len, DN_VAL_DIM), BF16),
                   jax.ShapeDtypeStruct((bsz, t_len, DN_VAL_DIM), BF16), state_shape, state_shape],
        scratch_shapes=[pltpu.VMEM((DN_V_HEADS, DN_HEAD, DN_HEAD), F32),
                        pltpu.VMEM((DN_V_HEADS, DN_HEAD, DN_HEAD), F32)],
        compiler_params=_cparams(2),
        name="dn_core",
    )(q, k, v, gb, q, k, v, gb, s0f, s0b)


def _dn_post_kernel(of_ref, ob_ref, z_ref, x_ref, mod_ref, nw_ref, wo_ref, g2_ref, wgu_ref, wd_ref,
                    *rest, final):
    if final:
        fg_ref, o_ref = rest
    else:
        fg_ref = None
        (o_ref,) = rest
    parts = []
    for hh in range(DN_V_HEADS):
        cols = slice(hh * DN_HEAD, (hh + 1) * DN_HEAD)
        oh = of_ref[0, :, cols].astype(F32) + ob_ref[0, :, cols].astype(F32)
        oh = oh * lax.rsqrt(jnp.mean(oh * oh, axis=-1, keepdims=True) + NORM_EPS) * nw_ref[...]
        parts.append((oh * _silu(z_ref[0, :, cols].astype(F32))).astype(BF16))
    y = _dot(jnp.concatenate(parts, axis=1), wo_ref[...])
    x1 = x_ref[0] + mod_ref[0, 2:3, :] * y
    o_ref[0] = _ffn_residual(x1, mod_ref, g2_ref, wgu_ref, wd_ref, fg_ref)


def _dn_post_call(of, ob, z, x, mod, nw, wo, g2, wgu, wd, final_g, *, tm, is_ctx):
    bsz, t_len, _ = x.shape
    final = final_g is not None
    in_specs = [_tok_spec(tm, DN_VAL_DIM), _tok_spec(tm, DN_VAL_DIM), _tok_spec(tm, DN_VAL_DIM),
                _tok_spec(tm, D_MODEL), _mod_spec(is_ctx), _whole(nw.shape), _whole(wo.shape),
                _whole((1, D_MODEL)), _whole(wgu.shape), _whole(wd.shape)]
    args = [of, ob, z, x, mod, nw, wo, g2, wgu, wd]
    if final:
        in_specs.append(_whole((1, D_MODEL)))
        args.append(final_g)
    return pl.pallas_call(
        functools.partial(_dn_post_kernel, final=final),
        grid=(bsz, t_len // tm),
        in_specs=in_specs,
        out_specs=_tok_spec(tm, D_MODEL),
        out_shape=jax.ShapeDtypeStruct(x.shape, F32),
        compiler_params=_cparams(2),
        name="dn_post",
    )(*args)


def _to_column_major(h):
    b, t_len, d = h.shape
    assert t_len == CHUNK * GRID_W
    h = h.reshape(b, SUBLANES, CHUNK_GROUPS, GRID_W, d)
    return h.transpose(0, 3, 2, 1, 4).reshape(b, t_len, d)


def _to_row_major(h):
    b, t_len, d = h.shape
    h = h.reshape(b, GRID_W, CHUNK_GROUPS, SUBLANES, d)
    return h.transpose(0, 3, 2, 1, 4).reshape(b, t_len, d)


def _interleave_chunks(h):
    b, t_len, d = h.shape
    h = h.reshape(b, t_len // CHUNK, SUBLANES, CHUNK_GROUPS, d)
    return h.transpose(0, 1, 3, 2, 4).reshape(b, t_len, d)


def _deinterleave_chunks(h):
    b, t_len, d = h.shape
    h = h.reshape(b, t_len // CHUNK, CHUNK_GROUPS, SUBLANES, d)
    return h.transpose(0, 1, 3, 2, 4).reshape(b, t_len, d)


def _pad_lanes(a, width=LANES):
    return jnp.pad(a, [(0, 0)] * (a.ndim - 1) + [(0, width - a.shape[-1])])


def kernel(x, c, ctx, c_ctx, ada_w, ada_b, norm_g, final_norm_g, rg_w_in, rg_b_in, rg_conv_w,
           rg_conv_b, rg_gate_w, rg_gate_b, rg_lambda, rg_w_out, rg_b_out, dn_w_in, dn_conv_w,
           dn_a_log, dn_dt_bias, dn_norm_w, dn_w_out, ffn_w_gu, ffn_w_down):
    bsz = x.shape[0]
    act = jnp.zeros((MOD_ROWS, D_MODEL), F32).at[0:bsz].set(c).at[CTX_MOD_ROW].set(c_ctx)
    mods = _ada_call(act, ada_w, ada_b).reshape(DEPTH, MOD_ROWS, N_MOD, D_MODEL)
    mods = jnp.pad(mods, ((0, 0), (0, 0), (0, MOD_ROWS - N_MOD), (0, 0)))
    zero_state = jnp.zeros((bsz, SUBLANES, D_MODEL), F32)
    zero_s = jnp.zeros((bsz, DN_V_HEADS, DN_HEAD, DN_HEAD), F32)

    for layer in range(DEPTH):
        last = layer == DEPTH - 1
        mod = mods[layer]
        g1 = norm_g[layer, 0].reshape(1, D_MODEL)
        g2 = norm_g[layer, 1].reshape(1, D_MODEL)
        wgu = ffn_w_gu[layer].astype(BF16)
        wd = ffn_w_down[layer].astype(BF16)
        j = layer // 2
        if layer % 2 == 0:
            w_in = rg_w_in[j].astype(BF16)
            b_in = rg_b_in[j].reshape(1, 2 * D_MODEL)
            cw = rg_conv_w[j]
            cb = rg_conv_b[j].reshape(1, D_MODEL)
            gw = (0.5 * rg_gate_w[j]).astype(BF16)
            gbias = 0.5 * rg_gate_b[j]
            lam = rg_lambda[j]
            wo = rg_w_out[j].astype(BF16)
            bo = rg_b_out[j].reshape(1, D_MODEL)

            def layer_fn(xs, h0f, h0b, tm_a, tm_b, is_ctx):
                hf, ab, bxb, y, hf_fin = _rg_a_call(xs, mod, g1, w_in, b_in, cw, cb, gw, gbias,
                                                    lam, h0f, tm=tm_a, is_ctx=is_ctx)
                out, hb_fin = _rg_post_call(xs, hf, ab, bxb, y, h0b, mod, wo, bo, g2, wgu, wd,
                                            tm=tm_b, is_ctx=is_ctx)
                return out, hf_fin, hb_fin

            ctx_new, hf_c, hb_c = layer_fn(ctx, zero_state, zero_state, TM_CTX, TM_CTX, True)
            x, _, _ = layer_fn(x, hf_c, hb_c, TM_PREP, TM_POST, False)
            ctx = ctx_new
        else:
            w_in = dn_w_in[j].astype(BF16)
            wba = _pad_lanes(w_in[:, DN_CONV_DIM + DN_VAL_DIM:])
            cw = 0.5 * dn_conv_w[j]
            n_bd = 2 * DN_V_HEADS
            alog = jnp.zeros((1, LANES), F32).at[0, n_bd:2 * n_bd].set(dn_a_log[j].reshape(-1))
            dtb = jnp.zeros((1, LANES), F32).at[0, n_bd:2 * n_bd].set(dn_dt_bias[j].reshape(-1))
            nw = dn_norm_w[j].reshape(1, DN_HEAD)
            wo = dn_w_out[j].astype(BF16)
            final_g = final_norm_g.reshape(1, D_MODEL) if last else None

            def prep(xs, tm, is_ctx):
                return _dn_prep_call(xs, mod, g1, w_in, wba, cw, alog, dtb, tm=tm, is_ctx=is_ctx)

            ctx_il = _interleave_chunks(ctx)
            qc, kc, vc, zc, gbc = prep(ctx_il, TM_CTX, True)
            ofc, obc, sf, sb = _dn_core_call(qc, kc, vc, gbc, zero_s, zero_s)
            x_cm = _to_column_major(x)
            q, k, v, z, gb = prep(x_cm, TM_PREP, False)
            of, ob, _, _ = _dn_core_call(q, k, v, gb, sf, sb)
            x_cm = _dn_post_call(of, ob, z, x_cm, mod, nw, wo, g2, wgu, wd, final_g,
                                 tm=TM_POST, is_ctx=False)
            x = _to_row_major(x_cm)
            if not last:
                ctx = _deinterleave_chunks(
                    _dn_post_call(ofc, obc, zc, ctx_il, mod, nw, wo, g2, wgu, wd, None,
                                  tm=TM_CTX, is_ctx=True))
    return x
```
